```python
import jax, jax.numpy as jnp
from jax import lax
import numpy as np

D_MODEL = 1024
BATCH = 8
SEQ = 4096
DEPTH = 1
DEC_BATCH = 2
DEC_SEQ = 8192
PAST_LEN = 128

N_META = 16
GRID_W = 64
EPS = 1e-6
NEG = -1e30
NA_HEADS = 8
NA_HEAD_DIM = 64
NA_WIDTH = NA_HEADS * NA_HEAD_DIM
NA_WIN_ROWS = 8
NA_WIN_COLS = 16
NA_QBLOCK = 16
NA_KCOLS = 32
ML_HEADS = 4
ML_HEAD_DIM = 128
ML_WIDTH = ML_HEADS * ML_HEAD_DIM
ML_CHUNK = 64
ML_CONV = 3
SPLITS = (NA_WIDTH, NA_WIDTH, NA_WIDTH, NA_WIDTH, 2 * ML_WIDTH, ML_WIDTH, ML_WIDTH, ML_WIDTH, 4 * ML_HEADS, 2 * D_MODEL)
N_IN = sum(SPLITS)
SPLIT_IDX = tuple(int(s) for s in np.cumsum(SPLITS)[:-1])
GATE_OFF = 4 * NA_WIDTH + 5 * ML_WIDTH

kernel_name = 'hybrid_na_mlstm_encoder'


def _rmsnorm(x, g):
    xf = x.astype(jnp.float32)
    y = xf * lax.rsqrt(jnp.mean(xf * xf, axis=-1, keepdims=True) + EPS)
    return (y * g.astype(jnp.float32)).astype(x.dtype)


def _na_column_tables():
    n_cb = GRID_W // NA_QBLOCK
    c = np.arange(GRID_W).reshape(n_cb, NA_QBLOCK)
    c0 = np.clip(c - NA_WIN_COLS // 2, 0, GRID_W - NA_WIN_COLS)
    kc0 = np.clip(np.arange(n_cb) * NA_QBLOCK - NA_WIN_COLS // 2, 0, GRID_W - NA_KCOLS)
    kc = kc0[:, None] + np.arange(NA_KCOLS)[None]
    valid = (kc[:, None, :] >= c0[..., None]) & (kc[:, None, :] < c0[..., None] + NA_WIN_COLS)
    dc = np.clip(kc[:, None, :] - c[..., None] + NA_WIN_COLS - 1, 0, 2 * NA_WIN_COLS - 2)
    return kc, valid, dc


def _neighbourhood_attention(q, k, v, rpb):
    B, L, H, dh = q.shape
    T = L - N_META
    rows = T // GRID_W
    wr = min(NA_WIN_ROWS, rows)
    n_cb = GRID_W // NA_QBLOCK
    scale = dh ** -0.5
    kc, valid, dc = _na_column_tables()
    qm, km, vm = q[:, :N_META], k[:, :N_META], v[:, :N_META]
    s_mm = jnp.einsum('bqhd,bkhd->bhqk', qm, km).astype(jnp.float32) * scale
    p_mm = jax.nn.softmax(s_mm, axis=-1).astype(v.dtype)
    out_meta = jnp.einsum('bhqk,bkhd->bqhd', p_mm, vm)
    qg = q[:, N_META:].reshape(B, rows, n_cb, NA_QBLOCK, H, dh)
    kg = k[:, N_META:].reshape(B, rows, GRID_W, H, dh)
    vg = v[:, N_META:].reshape(B, rows, GRID_W, H, dh)
    bias_tab = rpb.astype(jnp.float32)[:, :, dc]

    def row_block(r):
        r0 = jnp.clip(r - wr // 2, 0, rows - wr)
        k_blk = lax.dynamic_slice_in_dim(kg, r0, wr, axis=1)[:, :, kc]
        v_blk = lax.dynamic_slice_in_dim(vg, r0, wr, axis=1)[:, :, kc]
        q_blk = lax.dynamic_index_in_dim(qg, r, axis=1, keepdims=False)
        s_win = jnp.einsum('bcqhd,bwckhd->bhcqwk', q_blk, k_blk).astype(jnp.float32) * scale
        dr = r0 + jnp.arange(wr) - r + (NA_WIN_ROWS - 1)
        bias = jnp.take(bias_tab, dr, axis=1).transpose(0, 2, 3, 1, 4)
        s_win = jnp.where(valid[:, :, None, :], s_win + bias, NEG)
        s_win = s_win.reshape(B, H, n_cb, NA_QBLOCK, wr * NA_KCOLS)
        s_met = jnp.einsum('bcqhd,bmhd->bhcqm', q_blk, km).astype(jnp.float32) * scale
        p = jax.nn.softmax(jnp.concatenate([s_met, s_win], axis=-1), axis=-1).astype(v.dtype)
        p_met = p[..., :N_META]
        p_win = p[..., N_META:].reshape(B, H, n_cb, NA_QBLOCK, wr, NA_KCOLS)
        return (jnp.einsum('bhcqm,bmhd->bcqhd', p_met, vm)
                + jnp.einsum('bhcqwk,bwckhd->bcqhd', p_win, v_blk))

    out = lax.map(row_block, jnp.arange(rows))
    out_grid = jnp.moveaxis(out, 0, 1).reshape(B, T, H, dh)
    return jnp.concatenate([out_meta, out_grid], axis=1)


def _mlstm_chunkwise(q, k, v, ig, lf):
    B, H, L, d = q.shape
    nc = L // ML_CHUNK
    q = q.reshape(B, H, nc, ML_CHUNK, d)
    k = k.reshape(B, H, nc, ML_CHUNK, d)
    v = v.reshape(B, H, nc, ML_CHUNK, d)
    ig = ig.reshape(B, H, nc, ML_CHUNK)
    lf = lf.reshape(B, H, nc, ML_CHUNK)
    b = jnp.cumsum(lf, axis=-1)
    g = b[..., -1]
    a = g[..., None] - b + ig
    m_loc = jnp.max(a, axis=-1)
    w = jnp.exp(a - m_loc[..., None])
    dC = jnp.einsum('bhnsv,bhnsk->bhnvk', w[..., None] * v, k)
    dn = jnp.einsum('bhns,bhnsk->bhnk', w, k)

    def step(carry, xs):
        C, n, m = carry
        g_c, ml_c, dC_c, dn_c = xs
        m_new = jnp.maximum(g_c + m, ml_c)
        fs = jnp.exp(g_c + m - m_new)
        isc = jnp.exp(ml_c - m_new)
        C_new = fs[..., None, None] * C + isc[..., None, None] * dC_c
        n_new = fs[..., None] * n + isc[..., None] * dn_c
        return (C_new, n_new, m_new), (C, n, m)

    init = (jnp.zeros((B, H, d, d), q.dtype), jnp.zeros((B, H, d), q.dtype), jnp.zeros((B, H), q.dtype))
    xs = (jnp.moveaxis(g, 2, 0), jnp.moveaxis(m_loc, 2, 0), jnp.moveaxis(dC, 2, 0), jnp.moveaxis(dn, 2, 0))
    _, (Cs, ns, ms) = lax.scan(step, init, xs)
    Cs = jnp.moveaxis(Cs, 0, 2)
    ns = jnp.moveaxis(ns, 0, 2)
    ms = jnp.moveaxis(ms, 0, 2)
    lower = np.tril(np.ones((ML_CHUNK, ML_CHUNK), dtype=bool))
    logD = jnp.where(lower, b[..., :, None] - b[..., None, :] + ig[..., None, :], NEG)
    inter = b + ms[..., None]
    m_t = jnp.maximum(jnp.max(logD, axis=-1), inter)
    S = jnp.einsum('bhnjd,bhnsd->bhnjs', q, k) * jnp.exp(logD - m_t[..., None])
    e_inter = jnp.exp(inter - m_t)
    num = jnp.einsum('bhnjs,bhnsv->bhnjv', S, v) + e_inter[..., None] * jnp.einsum('bhnvk,bhnjk->bhnjv', Cs, q)
    den = jnp.sum(S, axis=-1) + e_inter * jnp.einsum('bhnk,bhnjk->bhnj', ns, q)
    h = num / jnp.maximum(jnp.abs(den), jnp.exp(-m_t))[..., None]
    return h.reshape(B, H, L, d)


def _mlstm_branch(qk_pre, v, gate_pre, o_pre, conv_w, head_g):
    B, L, _ = v.shape
    f32 = jnp.float32
    cw = conv_w.astype(qk_pre.dtype)[:, None, :]
    qk = jax.nn.silu(lax.conv_general_dilated(qk_pre, cw, (1,), 'SAME',
                                              dimension_numbers=('NWC', 'WIO', 'NWC'),
                                              feature_group_count=qk_pre.shape[-1]))
    q, k = jnp.split(qk, 2, axis=-1)

    def heads(t):
        return t.astype(f32).reshape(B, L, ML_HEADS, ML_HEAD_DIM).transpose(0, 2, 1, 3)

    q, k, vh = heads(q), heads(k) * (ML_HEAD_DIM ** -0.5), heads(v)
    gates = gate_pre.astype(f32).reshape(B, L, 4, ML_HEADS).transpose(2, 0, 3, 1)
    ig_f, lf_f = gates[0], jax.nn.log_sigmoid(gates[1])
    ig_b, lf_b = gates[2], jax.nn.log_sigmoid(gates[3])
    P = ML_CHUNK - N_META

    def pad_t(t, val):
        return jnp.pad(t, [(0, 0), (0, 0), (P, 0)] + [(0, 0)] * (t.ndim - 3), constant_values=val)

    def flip(t):
        return jnp.flip(t, axis=2)

    qp, kp, vp = pad_t(q, 0.0), pad_t(k, 0.0), pad_t(vh, 0.0)
    h_f = _mlstm_chunkwise(qp, kp, vp, pad_t(ig_f, NEG), pad_t(lf_f, 0.0))
    h_b = flip(_mlstm_chunkwise(flip(qp), flip(kp), flip(vp), flip(pad_t(ig_b, NEG)), flip(pad_t(lf_b, 0.0))))
    h = (h_f + h_b)[:, :, P:].transpose(0, 2, 1, 3)
    h = jax.nn.sigmoid(o_pre.astype(f32)).reshape(B, L, ML_HEADS, ML_HEAD_DIM) * h
    mu = jnp.mean(h, axis=-1, keepdims=True)
    var = jnp.mean(jnp.square(h - mu), axis=-1, keepdims=True)
    h = (h - mu) * lax.rsqrt(var + EPS)
    return (h.reshape(B, L, ML_WIDTH) * head_g.astype(f32)).astype(v.dtype)


def _encoder_layer(h, g_pre, w_in, b_in, na_rpb, ml_conv_w, ml_head_g, w_a, w_b, w_out, g_post):
    B, L, _ = h.shape
    xn = _rmsnorm(h, g_pre)
    proj = xn @ w_in + b_in.astype(xn.dtype)
    na_q, na_k, na_v, na_z, ml_qk, ml_v, ml_z, ml_o, ml_g, mg = jnp.split(proj, SPLIT_IDX, axis=-1)
    hd = (B, L, NA_HEADS, NA_HEAD_DIM)
    ya = _neighbourhood_attention(na_q.reshape(hd), na_k.reshape(hd), na_v.reshape(hd), na_rpb)
    ya = (ya.reshape(B, L, NA_WIDTH) * jax.nn.silu(na_z)) @ w_a
    yb = (_mlstm_branch(ml_qk, ml_v, ml_g, ml_o, ml_conv_w, ml_head_g) * jax.nn.silu(ml_z)) @ w_b
    ga, gb = jnp.split(jax.nn.sigmoid(mg), 2, axis=-1)
    y = (ga * ya + gb * yb) @ w_out
    return h + _rmsnorm(y, g_post)


def _encoder_trunk(x, meta_tokens, g_pre, w_in, b_in, na_rpb, ml_conv_w, ml_head_g, w_a, w_b, w_out, g_post):
    B = x.shape[0]
    meta = jnp.broadcast_to(meta_tokens.astype(x.dtype)[None], (B, N_META, x.shape[-1]))
    h = jnp.concatenate([meta, x], axis=1)
    for l in range(DEPTH):
        h = _encoder_layer(h, g_pre[l], w_in[l], b_in[l], na_rpb[l], ml_conv_w[l], ml_head_g[l],
                           w_a[l], w_b[l], w_out[l], g_post[l])
    return h[:, N_META:]


def setup_inputs(seed: int = 0) -> dict:
    key = jax.random.key(seed)
    ks = jax.random.split(key, 16)
    f32 = jnp.float32
    nrm = jax.random.normal
    x_prompt = nrm(ks[0], (BATCH, SEQ, D_MODEL), f32)
    x_sample = nrm(ks[1], (DEC_BATCH, DEC_SEQ, D_MODEL), f32)
    meta_tokens = nrm(ks[2], (N_META, D_MODEL), f32)
    g_pre = 1.0 + 0.02 * nrm(ks[3], (DEPTH, D_MODEL), f32)
    w_in = nrm(ks[4], (DEPTH, D_MODEL, N_IN), f32) * (D_MODEL ** -0.5)
    b_in = 0.02 * nrm(ks[5], (DEPTH, N_IN), f32)
    fb = jnp.linspace(3.0, 6.0, ML_HEADS, dtype=f32)
    b_in = b_in.at[:, GATE_OFF + ML_HEADS:GATE_OFF + 2 * ML_HEADS].add(fb)
    b_in = b_in.at[:, GATE_OFF + 3 * ML_HEADS:GATE_OFF + 4 * ML_HEADS].add(fb)
    na_rpb = 0.1 * nrm(ks[6], (DEPTH, NA_HEADS, 2 * NA_WIN_ROWS - 1, 2 * NA_WIN_COLS - 1), f32)
    ml_conv_w = nrm(ks[7], (DEPTH, ML_CONV, 2 * ML_WIDTH), f32) * (ML_CONV ** -0.5)
    ml_head_g = 1.0 + 0.02 * nrm(ks[8], (DEPTH, ML_WIDTH), f32)
    w_a = nrm(ks[9], (DEPTH, NA_WIDTH, D_MODEL), f32) * (NA_WIDTH ** -0.5)
    w_b = nrm(ks[10], (DEPTH, ML_WIDTH, D_MODEL), f32) * (ML_WIDTH ** -0.5)
    w_out = nrm(ks[11], (DEPTH, D_MODEL, D_MODEL), f32) * (D_MODEL ** -0.5)
    g_post = 1.0 + 0.02 * nrm(ks[12], (DEPTH, D_MODEL), f32)
    return {'x_prompt': x_prompt, 'x_sample': x_sample, 'meta_tokens': meta_tokens, 'g_pre': g_pre,
            'w_in': w_in, 'b_in': b_in, 'na_rpb': na_rpb, 'ml_conv_w': ml_conv_w, 'ml_head_g': ml_head_g,
            'w_a': w_a, 'w_b': w_b, 'w_out': w_out, 'g_post': g_post}


def reference(x_prompt, x_sample, meta_tokens, g_pre, w_in, b_in, na_rpb, ml_conv_w, ml_head_g, w_a, w_b, w_out, g_post):
    y_prompt = _encoder_trunk(x_prompt, meta_tokens, g_pre, w_in, b_in, na_rpb, ml_conv_w, ml_head_g,
                              w_a, w_b, w_out, g_post)
    y_sample = _encoder_trunk(x_sample, meta_tokens, g_pre, w_in, b_in, na_rpb, ml_conv_w, ml_head_g,
                              w_a, w_b, w_out, g_post)
    return (y_prompt, y_sample)
```

```python
import functools

import jax
import jax.numpy as jnp
import numpy as np
from jax import lax
from jax.experimental import pallas as pl
from jax.experimental.pallas import tpu as pltpu

D_MODEL = 1024
N_META = 16
GRID_W = 64
EPS = 1e-6
NEG = -1e30
NA_HEADS = 8
NA_DH = 64
NA_WIDTH = NA_HEADS * NA_DH
NA_WIN_ROWS = 8
NA_WIN_COLS = 16
ML_HEADS = 4
ML_DH = 128
ML_WIDTH = ML_HEADS * ML_DH
N_GATES = 4 * ML_HEADS
ML_CHUNK = 128
ML_BLOCK = 256
PROJ_ROWS = 256
VMEM_LIMIT_BYTES = 56 * 1024 * 1024

F32 = jnp.float32
BF16 = jnp.bfloat16


def _dot(a, b):
    return jnp.dot(a, b, preferred_element_type=F32)


def _dot_nt(a, b):
    return lax.dot_general(a, b, (((1,), (1,)), ((), ())), preferred_element_type=F32)


def _dot_tn(a, b):
    return lax.dot_general(a, b, (((0,), (0,)), ((), ())), preferred_element_type=F32)


def _dot_exact(a, b):
    return jnp.dot(a, b, preferred_element_type=F32, precision=lax.Precision.HIGHEST)


def _silu(x):
    return x * jax.nn.sigmoid(x)


def _log_sigmoid(x):
    return jnp.minimum(x, 0.0) - jnp.log1p(jnp.exp(-jnp.abs(x)))


def _log2(n):
    assert n & (n - 1) == 0
    return n.bit_length() - 1


def _params(n_grid_axes):
    return pltpu.CompilerParams(dimension_semantics=("arbitrary",) * n_grid_axes,
                                vmem_limit_bytes=VMEM_LIMIT_BYTES)


def _proj_kernel(x_ref, g_ref, wna_ref, wqk_ref, wvzo_ref, wmg_ref, wg_ref, wgt_ref,
                 bna_ref, bqk_ref, bvzo_ref, bmg_ref, bg_ref, bgt_ref,
                 na_ref, qk_ref, vzo_ref, mg_ref, gc_ref, gr_ref, *, chunk):
    x = x_ref[...]
    xn = x * lax.rsqrt(jnp.mean(x * x, axis=-1, keepdims=True) + EPS) * g_ref[...]
    xb = xn.astype(BF16)
    na_ref[...] = (_dot(xb, wna_ref[...]) + bna_ref[...]).astype(BF16)
    qk_ref[...] = _dot(xb, wqk_ref[...]) + bqk_ref[...]
    vzo_ref[...] = (_dot(xb, wvzo_ref[...]) + bvzo_ref[...]).astype(BF16)
    mg_ref[...] = (_dot(xb, wmg_ref[...]) + bmg_ref[...]).astype(BF16)

    tm = x.shape[0]
    gc = _dot(xb, wg_ref[...]) + bg_ref[...]
    gr = _dot_nt(wgt_ref[...], xb) + bgt_ref[...]
    kind_c = lax.broadcasted_iota(jnp.int32, gc.shape, 1) >> _log2(ML_HEADS)
    kind_r = lax.broadcasted_iota(jnp.int32, gr.shape, 0) >> _log2(ML_HEADS)
    lfc = jnp.where((kind_c & 1) == 1, _log_sigmoid(gc), gc)
    lfr = jnp.where((kind_r & 1) == 1, _log_sigmoid(gr), gr)
    if chunk is None:
        gc_ref[...] = lfc
        gr_ref[...] = lfr
        return
    ri = lax.broadcasted_iota(jnp.int32, (tm, tm), 0)
    ci = lax.broadcasted_iota(jnp.int32, (tm, tm), 1)
    same = (ri >> _log2(chunk)) == (ci >> _log2(chunk))
    lower = jnp.where(same & (ci <= ri), 1.0, 0.0).astype(F32)
    upper = jnp.where(same & (ci >= ri), 1.0, 0.0).astype(F32)
    pre_c = _dot_exact(lower, lfc)
    suf_c = _dot_exact(upper, lfc)
    gc_ref[...] = jnp.where(kind_c == 1, pre_c, jnp.where(kind_c == 3, suf_c, gc))
    pre_r = _dot_exact(lfr, upper)
    suf_r = _dot_exact(lfr, lower)
    gr_ref[...] = jnp.where(kind_r == 1, pre_r, jnp.where(kind_r == 3, suf_r, gr))


def _proj(x2d, g_pre, wts, *, tm, chunk):
    rows = x2d.shape[0]
    assert rows % tm == 0
    wna, wqk, wvzo, wmg, wg, wgt, bna, bqk, bvzo, bmg, bg, bgt = wts
    const = lambda i: (0, 0)
    row = lambda i: (i, 0)

    def full(a):
        return pl.BlockSpec(a.shape, const)

    in_specs = [pl.BlockSpec((tm, D_MODEL), row), full(g_pre)] + [full(a) for a in wts]
    out_shape = (
        jax.ShapeDtypeStruct((rows, 4 * NA_WIDTH), BF16),
        jax.ShapeDtypeStruct((rows, 2 * ML_WIDTH), F32),
        jax.ShapeDtypeStruct((rows, 3 * ML_WIDTH), BF16),
        jax.ShapeDtypeStruct((rows, 2 * D_MODEL), BF16),
        jax.ShapeDtypeStruct((rows, N_GATES), F32),
        jax.ShapeDtypeStruct((N_GATES, rows), F32),
    )
    out_specs = (
        pl.BlockSpec((tm, 4 * NA_WIDTH), row),
        pl.BlockSpec((tm, 2 * ML_WIDTH), row),
        pl.BlockSpec((tm, 3 * ML_WIDTH), row),
        pl.BlockSpec((tm, 2 * D_MODEL), row),
        pl.BlockSpec((tm, N_GATES), row),
        pl.BlockSpec((N_GATES, tm), lambda i: (0, i)),
    )
    return pl.pallas_call(
        functools.partial(_proj_kernel, chunk=chunk),
        grid=(rows // tm,),
        in_specs=in_specs,
        out_specs=out_specs,
        out_shape=out_shape,
        compiler_params=_params(1),
        name="proj",
    )(x2d, g_pre, *wts)


def _na_row_start(r, n_rows):
    return jnp.clip(r - NA_WIN_ROWS // 2, 0, n_rows - NA_WIN_ROWS)


def _na_kernel(q_ref, k_ref, v_ref, z_ref, mk_ref, mv_ref, bias_ref, o_ref, *, n_rows):
    r = pl.program_id(1)
    r0 = _na_row_start(r, n_rows)
    start = pl.multiple_of(r0 * GRID_W, GRID_W)
    n_keys = NA_WIN_ROWS * GRID_W
    scale = NA_DH ** -0.5
    for h in range(NA_HEADS):
        cols = slice(h * NA_DH, (h + 1) * NA_DH)
        qh = q_ref[:, cols]
        kw = k_ref[pl.ds(start, n_keys), cols]
        vw = v_ref[pl.ds(start, n_keys), cols]
        bias = bias_ref[0, h]
        s = jnp.where(bias > 0.5 * NEG, _dot_nt(qh, kw) * scale + bias, NEG)
        sm = _dot_nt(qh, mk_ref[:, cols]) * scale
        m = jnp.maximum(jnp.max(s, axis=-1, keepdims=True), jnp.max(sm, axis=-1, keepdims=True))
        p = jnp.exp(s - m)
        pm = jnp.exp(sm - m)
        denom = jnp.sum(p, axis=-1, keepdims=True) + jnp.sum(pm, axis=-1, keepdims=True)
        o = _dot(p.astype(BF16), vw) + _dot(pm.astype(BF16), mv_ref[:, cols])
        z = z_ref[:, cols].astype(F32)
        o_ref[:, cols] = (o / denom * _silu(z)).astype(BF16)


def _na_bias_table(rpb):
    c = np.arange(GRID_W)
    c0 = np.clip(c - NA_WIN_COLS // 2, 0, GRID_W - NA_WIN_COLS)
    kc = np.arange(GRID_W)
    valid = (kc[None, :] >= c0[:, None]) & (kc[None, :] < c0[:, None] + NA_WIN_COLS)
    dc = np.clip(kc[None, :] - c[:, None] + NA_WIN_COLS - 1, 0, 2 * NA_WIN_COLS - 2)
    off = np.arange(NA_WIN_ROWS)
    dr = off[:, None] + np.arange(NA_WIN_ROWS)[None, :]
    tab = rpb.astype(F32)[:, dr[:, None, :, None], dc[None, :, None, :]]
    tab = jnp.where(valid[None, None, :, None, :], tab, NEG)
    tab = jnp.transpose(tab, (1, 0, 2, 3, 4))
    return tab.reshape(NA_WIN_ROWS, NA_HEADS, GRID_W, NA_WIN_ROWS * GRID_W)


def _na(na, meta_na, bias_tab, *, batch, seq):
    n_rows = seq // GRID_W
    assert n_rows >= NA_WIN_ROWS and seq % GRID_W == 0

    def bias_idx(b, r):
        return (_na_row_start(r, n_rows) - r + NA_WIN_ROWS - 1, 0, 0, 0)

    return pl.pallas_call(
        functools.partial(_na_kernel, n_rows=n_rows),
        grid=(batch, n_rows),
        in_specs=[
            pl.BlockSpec((GRID_W, NA_WIDTH), lambda b, r: (b * n_rows + r, 0)),
            pl.BlockSpec((seq, NA_WIDTH), lambda b, r: (b, 1)),
            pl.BlockSpec((seq, NA_WIDTH), lambda b, r: (b, 2)),
            pl.BlockSpec((GRID_W, NA_WIDTH), lambda b, r: (b * n_rows + r, 3)),
            pl.BlockSpec((N_META, NA_WIDTH), lambda b, r: (0, 1)),
            pl.BlockSpec((N_META, NA_WIDTH), lambda b, r: (0, 2)),
            pl.BlockSpec((1, NA_HEADS, GRID_W, NA_WIN_ROWS * GRID_W), bias_idx),
        ],
        out_specs=pl.BlockSpec((GRID_W, NA_WIDTH), lambda b, r: (b * n_rows + r, 0)),
        out_shape=jax.ShapeDtypeStruct((batch * seq, NA_WIDTH), BF16),
        compiler_params=_params(2),
        name="na",
    )(na, na, na, na, meta_na, meta_na, bias_tab)


def _conv_silu(x_ref, prev8, next8, w, xs_ref):
    tb = x_ref.shape[0]
    xs_ref[0:8, :] = prev8
    xs_ref[8:tb + 8, :] = x_ref[...]
    xs_ref[tb + 8:tb + 16, :] = next8
    y = (w[0:1] * xs_ref[7:tb + 7, :] + w[1:2] * xs_ref[8:tb + 8, :]
         + w[2:3] * xs_ref[9:tb + 9, :])
    return _silu(y)


def _ml_chunk(q, k, v_ext, bcol, igcol, brow, igrow, ct, m, *, reverse):
    ch = q.shape[0]
    ri = lax.broadcasted_iota(jnp.int32, (ch, ch), 0)
    ci = lax.broadcasted_iota(jnp.int32, (ch, ch), 1)
    causal = (ci >= ri) if reverse else (ci <= ri)
    g = bcol[0:1] if reverse else bcol[ch - 1:ch]
    logd = jnp.where(causal, bcol - brow + igrow, NEG)
    inter = bcol + m
    m_t = jnp.maximum(jnp.max(logd, axis=1, keepdims=True), inter)
    s = _dot_nt(q, k) * jnp.exp(logd - m_t)
    e_inter = jnp.exp(inter - m_t)
    comb = _dot(s.astype(BF16), v_ext) + e_inter * _dot(q, ct.astype(BF16))
    num = comb[:, :ML_DH]
    den = comb[:, ML_DH:ML_DH + 1]
    h = num / jnp.maximum(jnp.abs(den), jnp.exp(-m_t))
    a = g - bcol + igcol
    m_loc = jnp.max(a, axis=0, keepdims=True)
    w = jnp.exp(a - m_loc)
    m_new = jnp.maximum(g + m, m_loc)
    fs = jnp.exp(g + m - m_new)
    isc = jnp.exp(m_loc - m_new)
    dct = _dot_tn(k, (w * v_ext.astype(F32)).astype(BF16))
    return h, fs * ct + isc * dct, m_new


def _ones_column(rows):
    lane = lax.broadcasted_iota(jnp.int32, (rows, ML_DH), 1)
    return jnp.where(lane == 0, 1.0, 0.0).astype(BF16)


def _ml_block(qp_ref, kp_ref, qprev, kprev, qnext, knext, v_ref, gc_ref, gr_ref, cw_ref,
              ct_ref, m_ref, xs_ref, emit, *, reverse):
    tb = qp_ref.shape[0]
    q_all = _conv_silu(qp_ref, qprev, qnext, cw_ref[:, 0:ML_WIDTH], xs_ref).astype(BF16)
    k_all = (_conv_silu(kp_ref, kprev, knext, cw_ref[:, ML_WIDTH:2 * ML_WIDTH], xs_ref)
             * (ML_DH ** -0.5)).astype(BF16)
    n_chunks = tb // ML_CHUNK
    order = range(n_chunks - 1, -1, -1) if reverse else range(n_chunks)
    ig_i, b_i = (2 * ML_HEADS, 3 * ML_HEADS) if reverse else (0, ML_HEADS)
    ones = _ones_column(ML_CHUNK)
    for hd in range(ML_HEADS):
        cols = slice(hd * ML_DH, (hd + 1) * ML_DH)
        ct = ct_ref[hd]
        m = m_ref[hd][0:1, 0:1]
        for c in order:
            rows = slice(c * ML_CHUNK, (c + 1) * ML_CHUNK)
            v_ext = jnp.concatenate([v_ref[rows, cols], ones], axis=1)
            h, ct, m = _ml_chunk(
                q_all[rows, cols], k_all[rows, cols], v_ext,
                gc_ref[rows, b_i + hd:b_i + hd + 1], gc_ref[rows, ig_i + hd:ig_i + hd + 1],
                gr_ref[b_i + hd:b_i + hd + 1, rows], gr_ref[ig_i + hd:ig_i + hd + 1, rows],
                ct, m, reverse=reverse)
            emit(hd, rows, cols, h)
        ct_ref[hd] = ct
        m_ref[hd] = jnp.broadcast_to(m, m_ref.shape[1:])


def _ml_fwd_kernel(qp_ref, kp_ref, qprev_ref, kprev_ref, qnext_ref, knext_ref, v_ref, gc_ref, gr_ref,
                   mqk_ref, mv_ref, mgc_ref, mgr_ref, cw_ref, hf_ref, ct_ref, m_ref, xs_ref, *, n_blocks):
    i = pl.program_id(1)

    @pl.when(i == 0)
    def _():
        w = cw_ref[:, ML_WIDTH:2 * ML_WIDTH]
        xs_ref[0:8, :] = jnp.zeros((8, ML_WIDTH), F32)
        xs_ref[8:8 + N_META, :] = mqk_ref[:, ML_WIDTH:2 * ML_WIDTH]
        xs_ref[8 + N_META:16 + N_META, :] = kp_ref[0:8, :]
        y = (w[0:1] * xs_ref[7:7 + N_META, :] + w[1:2] * xs_ref[8:8 + N_META, :]
             + w[2:3] * xs_ref[9:9 + N_META, :])
        mk = (_silu(y) * (ML_DH ** -0.5)).astype(BF16)
        ones = _ones_column(N_META)
        ri = lax.broadcasted_iota(jnp.int32, (N_META, N_META), 0)
        ci = lax.broadcasted_iota(jnp.int32, (N_META, N_META), 1)
        for hd in range(ML_HEADS):
            cols = slice(hd * ML_DH, (hd + 1) * ML_DH)
            ig = mgc_ref[:, hd:hd + 1]
            lf = mgr_ref[ML_HEADS + hd:ML_HEADS + hd + 1, :]
            a = jnp.sum(jnp.where(ci > ri, lf, 0.0), axis=1, keepdims=True) + ig
            g = jnp.sum(lf, axis=1, keepdims=True)
            m_loc = jnp.max(a, axis=0, keepdims=True)
            wgt = jnp.exp(a - m_loc)
            m0 = jnp.maximum(g, m_loc)
            v_ext = jnp.concatenate([mv_ref[:, cols], ones], axis=1)
            dct = _dot_tn(mk[:, cols], (wgt * v_ext.astype(F32)).astype(BF16))
            ct_ref[hd] = jnp.exp(m_loc - m0) * dct
            m_ref[hd] = jnp.broadcast_to(m0, m_ref.shape[1:])

    first = i == 0
    last = i == n_blocks - 1
    zeros8 = jnp.zeros((8, ML_WIDTH), F32)
    qprev = jnp.where(first, mqk_ref[8:16, 0:ML_WIDTH], qprev_ref[...])
    kprev = jnp.where(first, mqk_ref[8:16, ML_WIDTH:2 * ML_WIDTH], kprev_ref[...])
    qnext = jnp.where(last, zeros8, qnext_ref[...])
    knext = jnp.where(last, zeros8, knext_ref[...])

    def emit(hd, rows, cols, h):
        hf_ref[rows, cols] = h

    _ml_block(qp_ref, kp_ref, qprev, kprev, qnext, knext, v_ref, gc_ref, gr_ref, cw_ref,
              ct_ref, m_ref, xs_ref, emit, reverse=False)


def _ml_bwd_kernel(qp_ref, kp_ref, qprev_ref, kprev_ref, qnext_ref, knext_ref, v_ref, gc_ref, gr_ref,
                   mqk_ref, cw_ref, hf_ref, o_ref, z_ref, hg_ref, y_ref, ct_ref, m_ref, xs_ref, *, n_blocks):
    j = pl.program_id(1)
    i = n_blocks - 1 - j

    @pl.when(j == 0)
    def _():
        ct_ref[...] = jnp.zeros(ct_ref.shape, F32)
        m_ref[...] = jnp.zeros(m_ref.shape, F32)

    first = i == 0
    last = i == n_blocks - 1
    zeros8 = jnp.zeros((8, ML_WIDTH), F32)
    qprev = jnp.where(first, mqk_ref[8:16, 0:ML_WIDTH], qprev_ref[...])
    kprev = jnp.where(first, mqk_ref[8:16, ML_WIDTH:2 * ML_WIDTH], kprev_ref[...])
    qnext = jnp.where(last, zeros8, qnext_ref[...])
    knext = jnp.where(last, zeros8, knext_ref[...])

    def emit(hd, rows, cols, h):
        hh = jax.nn.sigmoid(o_ref[rows, cols].astype(F32)) * (hf_ref[rows, cols] + h)
        mu = jnp.mean(hh, axis=-1, keepdims=True)
        d = hh - mu
        var = jnp.mean(d * d, axis=-1, keepdims=True)
        y = d * lax.rsqrt(var + EPS) * hg_ref[:, cols] * _silu(z_ref[rows, cols].astype(F32))
        y_ref[rows, cols] = y.astype(BF16)

    _ml_block(qp_ref, kp_ref, qprev, kprev, qnext, knext, v_ref, gc_ref, gr_ref, cw_ref,
              ct_ref, m_ref, xs_ref, emit, reverse=True)


def _ml_specs(batch, seq, tb, blk):
    nb = seq // tb
    n8 = tb // 8
    total8 = batch * seq // 8

    def main(col):
        return pl.BlockSpec((tb, ML_WIDTH), lambda b, j: (b * nb + blk(b, j), col))

    def prev(col):
        return pl.BlockSpec((8, ML_WIDTH), lambda b, j: (jnp.maximum((b * nb + blk(b, j)) * n8 - 1, 0), col))

    def nxt(col):
        return pl.BlockSpec((8, ML_WIDTH),
                            lambda b, j: (jnp.minimum((b * nb + blk(b, j) + 1) * n8, total8 - 1), col))

    gc = pl.BlockSpec((tb, N_GATES), lambda b, j: (b * nb + blk(b, j), 0))
    gr = pl.BlockSpec((N_GATES, tb), lambda b, j: (0, b * nb + blk(b, j)))
    return main, prev, nxt, gc, gr


def _ml_scratch(tb):
    return [pltpu.VMEM((ML_HEADS, ML_DH, 2 * ML_DH), F32),
            pltpu.VMEM((ML_HEADS, 8, 128), F32),
            pltpu.VMEM((tb + 16, ML_WIDTH), F32)]


def _ml_fwd(qk, vzo, gc, gr, meta_qk, meta_vzo, meta_gc, meta_gr, conv_w, *, batch, seq, tb):
    nb = seq // tb
    main, prev, nxt, gc_spec, gr_spec = _ml_specs(batch, seq, tb, lambda b, j: j)
    const = lambda b, j: (0, 0)
    return pl.pallas_call(
        functools.partial(_ml_fwd_kernel, n_blocks=nb),
        grid=(batch, nb),
        in_specs=[main(0), main(1), prev(0), prev(1), nxt(0), nxt(1), main(0), gc_spec, gr_spec,
                  pl.BlockSpec(meta_qk.shape, const),
                  pl.BlockSpec((N_META, ML_WIDTH), const),
                  pl.BlockSpec(meta_gc.shape, const),
                  pl.BlockSpec(meta_gr.shape, const),
                  pl.BlockSpec(conv_w.shape, const)],
        out_specs=main(0),
        out_shape=jax.ShapeDtypeStruct((batch * seq, ML_WIDTH), F32),
        scratch_shapes=_ml_scratch(tb),
        compiler_params=_params(2),
        name="ml_fwd",
    )(qk, qk, qk, qk, qk, qk, vzo, gc, gr, meta_qk, meta_vzo, meta_gc, meta_gr, conv_w)


def _ml_bwd(qk, vzo, gc, gr, meta_qk, conv_w, hf, head_g, *, batch, seq, tb):
    nb = seq // tb
    main, prev, nxt, gc_spec, gr_spec = _ml_specs(batch, seq, tb, lambda b, j: nb - 1 - j)
    const = lambda b, j: (0, 0)
    return pl.pallas_call(
        functools.partial(_ml_bwd_kernel, n_blocks=nb),
        grid=(batch, nb),
        in_specs=[main(0), main(1), prev(0), prev(1), nxt(0), nxt(1), main(0), gc_spec, gr_spec,
                  pl.BlockSpec(meta_qk.shape, const),
                  pl.BlockSpec(conv_w.shape, const),
                  main(0), main(2), main(1),
                  pl.BlockSpec(head_g.shape, const)],
        out_specs=main(0),
        out_shape=jax.ShapeDtypeStruct((batch * seq, ML_WIDTH), BF16),
        scratch_shapes=_ml_scratch(tb),
        compiler_params=_params(2),
        name="ml_bwd",
    )(qk, qk, qk, qk, qk, qk, vzo, gc, gr, meta_qk, conv_w, hf, vzo, vzo, head_g)


def _out_kernel(x_ref, ya_ref, yb_ref, mg_ref, wa_ref, wb_ref, wo_ref, g_ref, o_ref):
    ya = _dot(ya_ref[...], wa_ref[...])
    yb = _dot(yb_ref[...], wb_ref[...])
    ga = jax.nn.sigmoid(mg_ref[:, 0:D_MODEL].astype(F32))
    gb = jax.nn.sigmoid(mg_ref[:, D_MODEL:2 * D_MODEL].astype(F32))
    y = _dot((ga * ya + gb * yb).astype(BF16), wo_ref[...])
    yn = y * lax.rsqrt(jnp.mean(y * y, axis=-1, keepdims=True) + EPS) * g_ref[...]
    o_ref[...] = x_ref[...] + yn


def _out(x2d, ya, yb, mg, w_a, w_b, w_out, g_post, *, tm):
    rows = x2d.shape[0]
    row = lambda i: (i, 0)
    const = lambda i: (0, 0)
    return pl.pallas_call(
        _out_kernel,
        grid=(rows // tm,),
        in_specs=[pl.BlockSpec((tm, D_MODEL), row),
                  pl.BlockSpec((tm, NA_WIDTH), row),
                  pl.BlockSpec((tm, ML_WIDTH), row),
                  pl.BlockSpec((tm, 2 * D_MODEL), row),
                  pl.BlockSpec(w_a.shape, const),
                  pl.BlockSpec(w_b.shape, const),
                  pl.BlockSpec(w_out.shape, const),
                  pl.BlockSpec(g_post.shape, const)],
        out_specs=pl.BlockSpec((tm, D_MODEL), row),
        out_shape=jax.ShapeDtypeStruct((rows, D_MODEL), F32),
        compiler_params=_params(1),
        name="out",
    )(x2d, ya, yb, mg, w_a, w_b, w_out, g_post)


def _split_in_proj(w_in, b_in):
    o_qk = 4 * NA_WIDTH
    o_vzo = o_qk + 2 * ML_WIDTH
    o_g = o_vzo + 3 * ML_WIDTH
    o_mg = o_g + N_GATES
    wb = w_in.astype(BF16)
    b = b_in.astype(F32)[None, :]
    wg = wb[:, o_g:o_mg]
    bg = b[:, o_g:o_mg]
    return (wb[:, :o_qk], wb[:, o_qk:o_vzo], wb[:, o_vzo:o_g], wb[:, o_mg:], wg, wg.T,
            b[:, :o_qk], b[:, o_qk:o_vzo], b[:, o_vzo:o_g], b[:, o_mg:], bg, bg.T)


def _trunk(x, meta_proj, wts, bias_tab, g_pre, conv_w, head_g, w_a, w_b, w_out, g_post):
    batch, seq, _ = x.shape
    assert seq % ML_BLOCK == 0 and seq % PROJ_ROWS == 0 and ML_BLOCK % ML_CHUNK == 0
    assert PROJ_ROWS % ML_CHUNK == 0
    x2d = x.reshape(batch * seq, D_MODEL)
    m_na, m_qk, m_vzo, _, m_gc, m_gr = meta_proj
    na, qk, vzo, mg, gc, gr = _proj(x2d, g_pre, wts, tm=PROJ_ROWS, chunk=ML_CHUNK)
    ya = _na(na, m_na, bias_tab, batch=batch, seq=seq)
    hf = _ml_fwd(qk, vzo, gc, gr, m_qk, m_vzo, m_gc, m_gr, conv_w, batch=batch, seq=seq, tb=ML_BLOCK)
    yb = _ml_bwd(qk, vzo, gc, gr, m_qk, conv_w, hf, head_g, batch=batch, seq=seq, tb=ML_BLOCK)
    y = _out(x2d, ya, yb, mg, w_a, w_b, w_out, g_post, tm=PROJ_ROWS)
    return y.reshape(batch, seq, D_MODEL)


def kernel(x_prompt, x_sample, meta_tokens, g_pre, w_in, b_in, na_rpb, ml_conv_w, ml_head_g, w_a, w_b, w_out, g_post):
    assert w_in.shape[0] == 1, "the meta-token outputs are only droppable for a single layer"
    wts = _split_in_proj(w_in[0], b_in[0])
    gp = g_pre[0].astype(F32)[None, :]
    meta_proj = _proj(meta_tokens.astype(F32), gp, wts, tm=N_META, chunk=None)
    bias_tab = _na_bias_table(na_rpb[0])
    args = (meta_proj, wts, bias_tab, gp, ml_conv_w[0].astype(F32), ml_head_g[0].astype(F32)[None, :],
            w_a[0].astype(BF16), w_b[0].astype(BF16), w_out[0].astype(BF16), g_post[0].astype(F32)[None, :])
    return (_trunk(x_prompt, *args), _trunk(x_sample, *args))
```

```python
import functools

import jax
import jax.numpy as jnp
import numpy as np
from jax import lax
from jax.experimental import pallas as pl
from jax.experimental.pallas import tpu as pltpu

D_MODEL = 1024
N_META = 16
GRID_W = 64
EPS = 1e-6
NEG = -1e30
NA_HEADS = 8
NA_DH = 64
NA_WIDTH = NA_HEADS * NA_DH
NA_PAIR = 2 * NA_DH
NA_WIN_ROWS = 8
NA_WIN_COLS = 16
ML_HEADS = 4
ML_DH = 128
ML_WIDTH = ML_HEADS * ML_DH
N_GATES = 4 * ML_HEADS
ML_CHUNK = 128
ML_BLOCK = 256
PROJ_ROWS = 256
VMEM_LIMIT_BYTES = 56 * 1024 * 1024

F32 = jnp.float32
BF16 = jnp.bfloat16


def _dot(a, b):
    return jnp.dot(a, b, preferred_element_type=F32)


def _dot_nt(a, b):
    return lax.dot_general(a, b, (((1,), (1,)), ((), ())), preferred_element_type=F32)


def _dot_tn(a, b):
    return lax.dot_general(a, b, (((0,), (0,)), ((), ())), preferred_element_type=F32)


def _dot_exact(a, b):
    return jnp.dot(a, b, preferred_element_type=F32, precision=lax.Precision.HIGHEST)


def _silu(x):
    return x * jax.nn.sigmoid(x)


def _log_sigmoid(x):
    return jnp.minimum(x, 0.0) - jnp.log1p(jnp.exp(-jnp.abs(x)))


def _log2(n):
    assert n & (n - 1) == 0
    return n.bit_length() - 1


def _params(n_grid_axes):
    return pltpu.CompilerParams(dimension_semantics=("arbitrary",) * n_grid_axes,
                                vmem_limit_bytes=VMEM_LIMIT_BYTES)


def _proj_kernel(x_ref, g_ref, wna_ref, wqk_ref, wvzo_ref, wmg_ref, wg_ref, wgt_ref,
                 bna_ref, bqk_ref, bvzo_ref, bmg_ref, bg_ref, bgt_ref,
                 na_ref, qk_ref, vzo_ref, mg_ref, gc_ref, gr_ref, *, chunk):
    x = x_ref[...]
    xn = x * lax.rsqrt(jnp.mean(x * x, axis=-1, keepdims=True) + EPS) * g_ref[...]
    xb = xn.astype(BF16)
    na_ref[...] = (_dot(xb, wna_ref[...]) + bna_ref[...]).astype(BF16)
    qk_ref[...] = _dot(xb, wqk_ref[...]) + bqk_ref[...]
    vzo_ref[...] = (_dot(xb, wvzo_ref[...]) + bvzo_ref[...]).astype(BF16)
    mg_ref[...] = (_dot(xb, wmg_ref[...]) + bmg_ref[...]).astype(BF16)

    tm = x.shape[0]
    gc = _dot(xb, wg_ref[...]) + bg_ref[...]
    gr = _dot_nt(wgt_ref[...], xb) + bgt_ref[...]
    kind_c = lax.broadcasted_iota(jnp.int32, gc.shape, 1) >> _log2(ML_HEADS)
    kind_r = lax.broadcasted_iota(jnp.int32, gr.shape, 0) >> _log2(ML_HEADS)
    lfc = jnp.where((kind_c & 1) == 1, _log_sigmoid(gc), gc)
    lfr = jnp.where((kind_r & 1) == 1, _log_sigmoid(gr), gr)
    if chunk is None:
        gc_ref[...] = lfc
        gr_ref[...] = lfr
        return
    ri = lax.broadcasted_iota(jnp.int32, (tm, tm), 0)
    ci = lax.broadcasted_iota(jnp.int32, (tm, tm), 1)
    same = (ri >> _log2(chunk)) == (ci >> _log2(chunk))
    lower = jnp.where(same & (ci <= ri), 1.0, 0.0).astype(F32)
    upper = jnp.where(same & (ci >= ri), 1.0, 0.0).astype(F32)
    pre_c = _dot_exact(lower, lfc)
    suf_c = _dot_exact(upper, lfc)
    gc_ref[...] = jnp.where(kind_c == 1, pre_c, jnp.where(kind_c == 3, suf_c, gc))
    pre_r = _dot_exact(lfr, upper)
    suf_r = _dot_exact(lfr, lower)
    gr_ref[...] = jnp.where(kind_r == 1, pre_r, jnp.where(kind_r == 3, suf_r, gr))


def _proj(x2d, g_pre, wts, *, tm, chunk):
    rows = x2d.shape[0]
    assert rows % tm == 0
    wna, wqk, wvzo, wmg, wg, wgt, bna, bqk, bvzo, bmg, bg, bgt = wts
    const = lambda i: (0, 0)
    row = lambda i: (i, 0)

    def full(a):
        return pl.BlockSpec(a.shape, const)

    in_specs = [pl.BlockSpec((tm, D_MODEL), row), full(g_pre)] + [full(a) for a in wts]
    out_shape = (
        jax.ShapeDtypeStruct((rows, 4 * NA_WIDTH), BF16),
        jax.ShapeDtypeStruct((rows, 2 * ML_WIDTH), F32),
        jax.ShapeDtypeStruct((rows, 3 * ML_WIDTH), BF16),
        jax.ShapeDtypeStruct((rows, 2 * D_MODEL), BF16),
        jax.ShapeDtypeStruct((rows, N_GATES), F32),
        jax.ShapeDtypeStruct((N_GATES, rows), F32),
    )
    out_specs = (
        pl.BlockSpec((tm, 4 * NA_WIDTH), row),
        pl.BlockSpec((tm, 2 * ML_WIDTH), row),
        pl.BlockSpec((tm, 3 * ML_WIDTH), row),
        pl.BlockSpec((tm, 2 * D_MODEL), row),
        pl.BlockSpec((tm, N_GATES), row),
        pl.BlockSpec((N_GATES, tm), lambda i: (0, i)),
    )
    return pl.pallas_call(
        functools.partial(_proj_kernel, chunk=chunk),
        grid=(rows // tm,),
        in_specs=in_specs,
        out_specs=out_specs,
        out_shape=out_shape,
        compiler_params=_params(1),
        name="proj",
    )(x2d, g_pre, *wts)


def _na_row_start(r, n_rows):
    return jnp.clip(r - NA_WIN_ROWS // 2, 0, n_rows - NA_WIN_ROWS)


def _na_kernel(q_ref, k_ref, v_ref, z_ref, mk_ref, mv_ref, bias_ref, o_ref, *, n_rows):
    r = pl.program_id(1)
    r0 = _na_row_start(r, n_rows)
    start = pl.multiple_of(r0 * GRID_W, GRID_W)
    n_keys = NA_WIN_ROWS * GRID_W
    scale = NA_DH ** -0.5
    lane = lax.broadcasted_iota(jnp.int32, (GRID_W, NA_PAIR), 1)
    lo = lane < NA_DH
    n_pairs = NA_WIDTH // NA_PAIR
    groups = [slice(g * NA_PAIR, (g + 1) * NA_PAIR) for g in range(n_pairs)]

    scores = []
    for cols in groups:
        qf = q_ref[:, cols].astype(F32) * scale
        q2 = jnp.concatenate([jnp.where(lo, qf, 0.0), jnp.where(lo, 0.0, qf)], axis=0).astype(BF16)
        s = _dot_nt(q2, k_ref[pl.ds(start, n_keys), cols])
        sm = _dot_nt(q2, mk_ref[:, cols])
        scores.append((s, sm))
    probs = []
    for g, (s, sm) in enumerate(scores):
        bias = bias_ref[0, g]
        s = jnp.where(bias > 0.5 * NEG, s + bias, NEG)
        m = jnp.maximum(jnp.max(s, axis=-1, keepdims=True), jnp.max(sm, axis=-1, keepdims=True))
        p = jnp.exp(s - m)
        pm = jnp.exp(sm - m)
        denom = jnp.sum(p, axis=-1, keepdims=True) + jnp.sum(pm, axis=-1, keepdims=True)
        probs.append((p.astype(BF16), pm.astype(BF16), denom))
    for cols, (p, pm, denom) in zip(groups, probs):
        o2 = _dot(p, v_ref[pl.ds(start, n_keys), cols]) + _dot(pm, mv_ref[:, cols])
        o2 = o2 / denom
        o = jnp.where(lo, o2[0:GRID_W], o2[GRID_W:2 * GRID_W])
        o_ref[:, cols] = (o * _silu(z_ref[:, cols].astype(F32))).astype(BF16)


def _na_bias_table(rpb):
    c = np.arange(GRID_W)
    c0 = np.clip(c - NA_WIN_COLS // 2, 0, GRID_W - NA_WIN_COLS)
    kc = np.arange(GRID_W)
    valid = (kc[None, :] >= c0[:, None]) & (kc[None, :] < c0[:, None] + NA_WIN_COLS)
    dc = kc[None, :] - c[:, None] + NA_WIN_COLS - 1
    n_dc = 2 * NA_WIN_COLS - 1
    n_dr = 2 * NA_WIN_ROWS - 1
    onehot = ((dc[None] == np.arange(n_dc)[:, None, None]) & valid[None]).astype(np.float32)
    toe = jnp.dot(rpb.astype(F32).reshape(NA_HEADS * n_dr, n_dc), onehot.reshape(n_dc, GRID_W * GRID_W),
                  precision=lax.Precision.HIGHEST)
    toe = jnp.where(valid.reshape(1, GRID_W * GRID_W), toe, NEG)
    toe = toe.reshape(NA_HEADS, n_dr, GRID_W, GRID_W)
    tab = jnp.stack([toe[:, off:off + NA_WIN_ROWS] for off in range(NA_WIN_ROWS)])
    tab = jnp.transpose(tab, (0, 1, 3, 2, 4))
    return tab.reshape(NA_WIN_ROWS, NA_WIDTH // NA_PAIR, 2 * GRID_W, NA_WIN_ROWS * GRID_W)


def _na(na, meta_na, bias_tab, *, batch, seq):
    n_rows = seq // GRID_W
    assert n_rows >= NA_WIN_ROWS and seq % GRID_W == 0

    def bias_idx(b, r):
        return (_na_row_start(r, n_rows) - r + NA_WIN_ROWS - 1, 0, 0, 0)

    return pl.pallas_call(
        functools.partial(_na_kernel, n_rows=n_rows),
        grid=(batch, n_rows),
        in_specs=[
            pl.BlockSpec((GRID_W, NA_WIDTH), lambda b, r: (b * n_rows + r, 0)),
            pl.BlockSpec((seq, NA_WIDTH), lambda b, r: (b, 1)),
            pl.BlockSpec((seq, NA_WIDTH), lambda b, r: (b, 2)),
            pl.BlockSpec((GRID_W, NA_WIDTH), lambda b, r: (b * n_rows + r, 3)),
            pl.BlockSpec((N_META, NA_WIDTH), lambda b, r: (0, 1)),
            pl.BlockSpec((N_META, NA_WIDTH), lambda b, r: (0, 2)),
            pl.BlockSpec((1, NA_WIDTH // NA_PAIR, 2 * GRID_W, NA_WIN_ROWS * GRID_W), bias_idx),
        ],
        out_specs=pl.BlockSpec((GRID_W, NA_WIDTH), lambda b, r: (b * n_rows + r, 0)),
        out_shape=jax.ShapeDtypeStruct((batch * seq, NA_WIDTH), BF16),
        compiler_params=_params(2),
        name="na",
    )(na, na, na, na, meta_na, meta_na, bias_tab)


def _conv_silu(x_ref, prev8, next8, w, xs_ref):
    tb = x_ref.shape[0]
    xs_ref[0:8, :] = prev8
    xs_ref[8:tb + 8, :] = x_ref[...]
    xs_ref[tb + 8:tb + 16, :] = next8
    y = (w[0:1] * xs_ref[7:tb + 7, :] + w[1:2] * xs_ref[8:tb + 8, :]
         + w[2:3] * xs_ref[9:tb + 9, :])
    return _silu(y)


def _ml_chunk(q, k, v_ext, bcol, igcol, brow, igrow, ct, m, *, reverse):
    ch = q.shape[0]
    ri = lax.broadcasted_iota(jnp.int32, (ch, ch), 0)
    ci = lax.broadcasted_iota(jnp.int32, (ch, ch), 1)
    causal = (ci >= ri) if reverse else (ci <= ri)
    g = bcol[0:1] if reverse else bcol[ch - 1:ch]
    logd = jnp.where(causal, bcol - brow + igrow, NEG)
    inter = bcol + m
    m_t = jnp.maximum(jnp.max(logd, axis=1, keepdims=True), inter)
    s = _dot_nt(q, k) * jnp.exp(logd - m_t)
    e_inter = jnp.exp(inter - m_t)
    comb = _dot(s.astype(BF16), v_ext) + e_inter * _dot(q, ct.astype(BF16))
    num = comb[:, :ML_DH]
    den = comb[:, ML_DH:ML_DH + 1]
    h = num / jnp.maximum(jnp.abs(den), jnp.exp(-m_t))
    a = g - bcol + igcol
    m_loc = jnp.max(a, axis=0, keepdims=True)
    w = jnp.exp(a - m_loc)
    m_new = jnp.maximum(g + m, m_loc)
    fs = jnp.exp(g + m - m_new)
    isc = jnp.exp(m_loc - m_new)
    dct = _dot_tn(k, (w * v_ext.astype(F32)).astype(BF16))
    return h, fs * ct + isc * dct, m_new


def _ones_column(rows):
    lane = lax.broadcasted_iota(jnp.int32, (rows, ML_DH), 1)
    return jnp.where(lane == 0, 1.0, 0.0).astype(BF16)


def _ml_block(qp_ref, kp_ref, qprev, kprev, qnext, knext, v_ref, gc_ref, gr_ref, cw_ref,
              ct_ref, m_ref, xs_ref, emit, *, reverse):
    tb = qp_ref.shape[0]
    q_all = _conv_silu(qp_ref, qprev, qnext, cw_ref[:, 0:ML_WIDTH], xs_ref).astype(BF16)
    k_all = (_conv_silu(kp_ref, kprev, knext, cw_ref[:, ML_WIDTH:2 * ML_WIDTH], xs_ref)
             * (ML_DH ** -0.5)).astype(BF16)
    n_chunks = tb // ML_CHUNK
    order = range(n_chunks - 1, -1, -1) if reverse else range(n_chunks)
    ig_i, b_i = (2 * ML_HEADS, 3 * ML_HEADS) if reverse else (0, ML_HEADS)
    ones = _ones_column(ML_CHUNK)
    for hd in range(ML_HEADS):
        cols = slice(hd * ML_DH, (hd + 1) * ML_DH)
        ct = ct_ref[hd]
        m = m_ref[hd][0:1, 0:1]
        for c in order:
            rows = slice(c * ML_CHUNK, (c + 1) * ML_CHUNK)
            v_ext = jnp.concatenate([v_ref[rows, cols], ones], axis=1)
            h, ct, m = _ml_chunk(
                q_all[rows, cols], k_all[rows, cols], v_ext,
                gc_ref[rows, b_i + hd:b_i + hd + 1], gc_ref[rows, ig_i + hd:ig_i + hd + 1],
                gr_ref[b_i + hd:b_i + hd + 1, rows], gr_ref[ig_i + hd:ig_i + hd + 1, rows],
                ct, m, reverse=reverse)
            emit(hd, rows, cols, h)
        ct_ref[hd] = ct
        m_ref[hd] = jnp.broadcast_to(m, m_ref.shape[1:])


def _ml_fwd_kernel(qp_ref, kp_ref, qprev_ref, kprev_ref, qnext_ref, knext_ref, v_ref, gc_ref, gr_ref,
                   mqk_ref, mv_ref, mgc_ref, mgr_ref, cw_ref, hf_ref, ct_ref, m_ref, xs_ref, *, n_blocks):
    i = pl.program_id(1)

    @pl.when(i == 0)
    def _():
        w = cw_ref[:, ML_WIDTH:2 * ML_WIDTH]
        xs_ref[0:8, :] = jnp.zeros((8, ML_WIDTH), F32)
        xs_ref[8:8 + N_META, :] = mqk_ref[:, ML_WIDTH:2 * ML_WIDTH]
        xs_ref[8 + N_META:16 + N_META, :] = kp_ref[0:8, :]
        y = (w[0:1] * xs_ref[7:7 + N_META, :] + w[1:2] * xs_ref[8:8 + N_META, :]
             + w[2:3] * xs_ref[9:9 + N_META, :])
        mk = (_silu(y) * (ML_DH ** -0.5)).astype(BF16)
        ones = _ones_column(N_META)
        ri = lax.broadcasted_iota(jnp.int32, (N_META, N_META), 0)
        ci = lax.broadcasted_iota(jnp.int32, (N_META, N_META), 1)
        for hd in range(ML_HEADS):
            cols = slice(hd * ML_DH, (hd + 1) * ML_DH)
            ig = mgc_ref[:, hd:hd + 1]
            lf = mgr_ref[ML_HEADS + hd:ML_HEADS + hd + 1, :]
            a = jnp.sum(jnp.where(ci > ri, lf, 0.0), axis=1, keepdims=True) + ig
            g = jnp.sum(lf, axis=1, keepdims=True)
            m_loc = jnp.max(a, axis=0, keepdims=True)
            wgt = jnp.exp(a - m_loc)
            m0 = jnp.maximum(g, m_loc)
            v_ext = jnp.concatenate([mv_ref[:, cols], ones], axis=1)
            dct = _dot_tn(mk[:, cols], (wgt * v_ext.astype(F32)).astype(BF16))
            ct_ref[hd] = jnp.exp(m_loc - m0) * dct
            m_ref[hd] = jnp.broadcast_to(m0, m_ref.shape[1:])

    first = i == 0
    last = i == n_blocks - 1
    zeros8 = jnp.zeros((8, ML_WIDTH), F32)
    qprev = jnp.where(first, mqk_ref[8:16, 0:ML_WIDTH], qprev_ref[...])
    kprev = jnp.where(first, mqk_ref[8:16, ML_WIDTH:2 * ML_WIDTH], kprev_ref[...])
    qnext = jnp.where(last, zeros8, qnext_ref[...])
    knext = jnp.where(last, zeros8, knext_ref[...])

    def emit(hd, rows, cols, h):
        hf_ref[rows, cols] = h

    _ml_block(qp_ref, kp_ref, qprev, kprev, qnext, knext, v_ref, gc_ref, gr_ref, cw_ref,
              ct_ref, m_ref, xs_ref, emit, reverse=False)


def _ml_bwd_kernel(qp_ref, kp_ref, qprev_ref, kprev_ref, qnext_ref, knext_ref, v_ref, gc_ref, gr_ref,
                   mqk_ref, cw_ref, hf_ref, o_ref, z_ref, hg_ref, y_ref, ct_ref, m_ref, xs_ref, *, n_blocks):
    j = pl.program_id(1)
    i = n_blocks - 1 - j

    @pl.when(j == 0)
    def _():
        ct_ref[...] = jnp.zeros(ct_ref.shape, F32)
        m_ref[...] = jnp.zeros(m_ref.shape, F32)

    first = i == 0
    last = i == n_blocks - 1
    zeros8 = jnp.zeros((8, ML_WIDTH), F32)
    qprev = jnp.where(first, mqk_ref[8:16, 0:ML_WIDTH], qprev_ref[...])
    kprev = jnp.where(first, mqk_ref[8:16, ML_WIDTH:2 * ML_WIDTH], kprev_ref[...])
    qnext = jnp.where(last, zeros8, qnext_ref[...])
    knext = jnp.where(last, zeros8, knext_ref[...])

    def emit(hd, rows, cols, h):
        hh = jax.nn.sigmoid(o_ref[rows, cols].astype(F32)) * (hf_ref[rows, cols] + h)
        mu = jnp.mean(hh, axis=-1, keepdims=True)
        d = hh - mu
        var = jnp.mean(d * d, axis=-1, keepdims=True)
        y = d * lax.rsqrt(var + EPS) * hg_ref[:, cols] * _silu(z_ref[rows, cols].astype(F32))
        y_ref[rows, cols] = y.astype(BF16)

    _ml_block(qp_ref, kp_ref, qprev, kprev, qnext, knext, v_ref, gc_ref, gr_ref, cw_ref,
              ct_ref, m_ref, xs_ref, emit, reverse=True)


def _ml_specs(batch, seq, tb, blk):
    nb = seq // tb
    n8 = tb // 8
    total8 = batch * seq // 8

    def main(col):
        return pl.BlockSpec((tb, ML_WIDTH), lambda b, j: (b * nb + blk(b, j), col))

    def prev(col):
        return pl.BlockSpec((8, ML_WIDTH), lambda b, j: (jnp.maximum((b * nb + blk(b, j)) * n8 - 1, 0), col))

    def nxt(col):
        return pl.BlockSpec((8, ML_WIDTH),
                            lambda b, j: (jnp.minimum((b * nb + blk(b, j) + 1) * n8, total8 - 1), col))

    gc = pl.BlockSpec((tb, N_GATES), lambda b, j: (b * nb + blk(b, j), 0))
    gr = pl.BlockSpec((N_GATES, tb), lambda b, j: (0, b * nb + blk(b, j)))
    return main, prev, nxt, gc, gr


def _ml_scratch(tb):
    return [pltpu.VMEM((ML_HEADS, ML_DH, 2 * ML_DH), F32),
            pltpu.VMEM((ML_HEADS, 8, 128), F32),
            pltpu.VMEM((tb + 16, ML_WIDTH), F32)]


def _ml_fwd(qk, vzo, gc, gr, meta_qk, meta_vzo, meta_gc, meta_gr, conv_w, *, batch, seq, tb):
    nb = seq // tb
    main, prev, nxt, gc_spec, gr_spec = _ml_specs(batch, seq, tb, lambda b, j: j)
    const = lambda b, j: (0, 0)
    return pl.pallas_call(
        functools.partial(_ml_fwd_kernel, n_blocks=nb),
        grid=(batch, nb),
        in_specs=[main(0), main(1), prev(0), prev(1), nxt(0), nxt(1), main(0), gc_spec, gr_spec,
                  pl.BlockSpec(meta_qk.shape, const),
                  pl.BlockSpec((N_META, ML_WIDTH), const),
                  pl.BlockSpec(meta_gc.shape, const),
                  pl.BlockSpec(meta_gr.shape, const),
                  pl.BlockSpec(conv_w.shape, const)],
        out_specs=main(0),
        out_shape=jax.ShapeDtypeStruct((batch * seq, ML_WIDTH), F32),
        scratch_shapes=_ml_scratch(tb),
        compiler_params=_params(2),
        name="ml_fwd",
    )(qk, qk, qk, qk, qk, qk, vzo, gc, gr, meta_qk, meta_vzo, meta_gc, meta_gr, conv_w)


def _ml_bwd(qk, vzo, gc, gr, meta_qk, conv_w, hf, head_g, *, batch, seq, tb):
    nb = seq // tb
    main, prev, nxt, gc_spec, gr_spec = _ml_specs(batch, seq, tb, lambda b, j: nb - 1 - j)
    const = lambda b, j: (0, 0)
    return pl.pallas_call(
        functools.partial(_ml_bwd_kernel, n_blocks=nb),
        grid=(batch, nb),
        in_specs=[main(0), main(1), prev(0), prev(1), nxt(0), nxt(1), main(0), gc_spec, gr_spec,
                  pl.BlockSpec(meta_qk.shape, const),
                  pl.BlockSpec(conv_w.shape, const),
                  main(0), main(2), main(1),
                  pl.BlockSpec(head_g.shape, const)],
        out_specs=main(0),
        out_shape=jax.ShapeDtypeStruct((batch * seq, ML_WIDTH), BF16),
        scratch_shapes=_ml_scratch(tb),
        compiler_params=_params(2),
        name="ml_bwd",
    )(qk, qk, qk, qk, qk, qk, vzo, gc, gr, meta_qk, conv_w, hf, vzo, vzo, head_g)


def _out_kernel(x_ref, ya_ref, yb_ref, mg_ref, wa_ref, wb_ref, wo_ref, g_ref, o_ref):
    ya = _dot(ya_ref[...], wa_ref[...])
    yb = _dot(yb_ref[...], wb_ref[...])
    ga = jax.nn.sigmoid(mg_ref[:, 0:D_MODEL].astype(F32))
    gb = jax.nn.sigmoid(mg_ref[:, D_MODEL:2 * D_MODEL].astype(F32))
    y = _dot((ga * ya + gb * yb).astype(BF16), wo_ref[...])
    yn = y * lax.rsqrt(jnp.mean(y * y, axis=-1, keepdims=True) + EPS) * g_ref[...]
    o_ref[...] = x_ref[...] + yn


def _out(x2d, ya, yb, mg, w_a, w_b, w_out, g_post, *, tm):
    rows = x2d.shape[0]
    row = lambda i: (i, 0)
    const = lambda i: (0, 0)
    return pl.pallas_call(
        _out_kernel,
        grid=(rows // tm,),
        in_specs=[pl.BlockSpec((tm, D_MODEL), row),
                  pl.BlockSpec((tm, NA_WIDTH), row),
                  pl.BlockSpec((tm, ML_WIDTH), row),
                  pl.BlockSpec((tm, 2 * D_MODEL), row),
                  pl.BlockSpec(w_a.shape, const),
                  pl.BlockSpec(w_b.shape, const),
                  pl.BlockSpec(w_out.shape, const),
                  pl.BlockSpec(g_post.shape, const)],
        out_specs=pl.BlockSpec((tm, D_MODEL), row),
        out_shape=jax.ShapeDtypeStruct((rows, D_MODEL), F32),
        compiler_params=_params(1),
        name="out",
    )(x2d, ya, yb, mg, w_a, w_b, w_out, g_post)


def _split_in_proj(w_in, b_in):
    o_qk = 4 * NA_WIDTH
    o_vzo = o_qk + 2 * ML_WIDTH
    o_g = o_vzo + 3 * ML_WIDTH
    o_mg = o_g + N_GATES
    wb = w_in.astype(BF16)
    b = b_in.astype(F32)[None, :]
    wg = wb[:, o_g:o_mg]
    bg = b[:, o_g:o_mg]
    return (wb[:, :o_qk], wb[:, o_qk:o_vzo], wb[:, o_vzo:o_g], wb[:, o_mg:], wg, wg.T,
            b[:, :o_qk], b[:, o_qk:o_vzo], b[:, o_vzo:o_g], b[:, o_mg:], bg, bg.T)


def _trunk(x, meta_proj, wts, bias_tab, g_pre, conv_w, head_g, w_a, w_b, w_out, g_post):
    batch, seq, _ = x.shape
    assert seq % ML_BLOCK == 0 and seq % PROJ_ROWS == 0 and ML_BLOCK % ML_CHUNK == 0
    assert PROJ_ROWS % ML_CHUNK == 0
    x2d = x.reshape(batch * seq, D_MODEL)
    m_na, m_qk, m_vzo, _, m_gc, m_gr = meta_proj
    na, qk, vzo, mg, gc, gr = _proj(x2d, g_pre, wts, tm=PROJ_ROWS, chunk=ML_CHUNK)
    ya = _na(na, m_na, bias_tab, batch=batch, seq=seq)
    hf = _ml_fwd(qk, vzo, gc, gr, m_qk, m_vzo, m_gc, m_gr, conv_w, batch=batch, seq=seq, tb=ML_BLOCK)
    yb = _ml_bwd(qk, vzo, gc, gr, m_qk, conv_w, hf, head_g, batch=batch, seq=seq, tb=ML_BLOCK)
    y = _out(x2d, ya, yb, mg, w_a, w_b, w_out, g_post, tm=PROJ_ROWS)
    return y.reshape(batch, seq, D_MODEL)


def kernel(x_prompt, x_sample, meta_tokens, g_pre, w_in, b_in, na_rpb, ml_conv_w, ml_head_g, w_a, w_b, w_out, g_post):
    assert w_in.shape[0] == 1, "the meta-token outputs are only droppable for a single layer"
    wts = _split_in_proj(w_in[0], b_in[0])
    gp = g_pre[0].astype(F32)[None, :]
    meta_proj = _proj(meta_tokens.astype(F32), gp, wts, tm=N_META, chunk=None)
    bias_tab = _na_bias_table(na_rpb[0])
    args = (meta_proj, wts, bias_tab, gp, ml_conv_w[0].astype(F32), ml_head_g[0].astype(F32)[None, :],
            w_a[0].astype(BF16), w_b[0].astype(BF16), w_out[0].astype(BF16), g_post[0].astype(F32)[None, :])
    return (_trunk(x_prompt, *args), _trunk(x_sample, *args))
```

```python
import functools

import jax
import jax.numpy as jnp
import numpy as np
from jax import lax
from jax.experimental import pallas as pl
from jax.experimental.pallas import tpu as pltpu

D_MODEL = 1024
N_META = 16
GRID_W = 64
EPS = 1e-6
NEG = -1e30
NA_HEADS = 8
NA_DH = 64
NA_WIDTH = NA_HEADS * NA_DH
NA_PAIR = 2 * NA_DH
NA_WIN_ROWS = 8
NA_WIN_COLS = 16
ML_HEADS = 4
ML_DH = 128
ML_WIDTH = ML_HEADS * ML_DH
N_GATES = 4 * ML_HEADS
ML_CHUNK = 128
ML_BLOCK = 256
PROJ_ROWS = 256
VMEM_LIMIT_BYTES = 56 * 1024 * 1024

F32 = jnp.float32
BF16 = jnp.bfloat16


def _dot(a, b):
    return jnp.dot(a, b, preferred_element_type=F32)


def _dot_nt(a, b):
    return lax.dot_general(a, b, (((1,), (1,)), ((), ())), preferred_element_type=F32)


def _dot_tn(a, b):
    return lax.dot_general(a, b, (((0,), (0,)), ((), ())), preferred_element_type=F32)


def _dot_exact(a, b):
    return jnp.dot(a, b, preferred_element_type=F32, precision=lax.Precision.HIGHEST)


def _silu(x):
    return x * jax.nn.sigmoid(x)


def _log_sigmoid(x):
    return jnp.minimum(x, 0.0) - jnp.log1p(jnp.exp(-jnp.abs(x)))


def _log2(n):
    assert n & (n - 1) == 0
    return n.bit_length() - 1


def _params(n_grid_axes):
    return pltpu.CompilerParams(dimension_semantics=("arbitrary",) * n_grid_axes,
                                vmem_limit_bytes=VMEM_LIMIT_BYTES)


def _proj_kernel(x_ref, g_ref, wna_ref, wqk_ref, wvzo_ref, wmg_ref, wg_ref, wgt_ref,
                 bna_ref, bqk_ref, bvzo_ref, bmg_ref, bg_ref, bgt_ref,
                 na_ref, qk_ref, vzo_ref, mg_ref, gc_ref, gr_ref, *, chunk):
    x = x_ref[...]
    xn = x * lax.rsqrt(jnp.mean(x * x, axis=-1, keepdims=True) + EPS) * g_ref[...]
    xb = xn.astype(BF16)
    na_ref[...] = (_dot(xb, wna_ref[...]) + bna_ref[...]).astype(BF16)
    qk_ref[...] = _dot(xb, wqk_ref[...]) + bqk_ref[...]
    vzo_ref[...] = (_dot(xb, wvzo_ref[...]) + bvzo_ref[...]).astype(BF16)
    mg_ref[...] = (_dot(xb, wmg_ref[...]) + bmg_ref[...]).astype(BF16)

    tm = x.shape[0]
    gc = _dot(xb, wg_ref[...]) + bg_ref[...]
    gr = _dot_nt(wgt_ref[...], xb) + bgt_ref[...]
    kind_c = lax.broadcasted_iota(jnp.int32, gc.shape, 1) >> _log2(ML_HEADS)
    kind_r = lax.broadcasted_iota(jnp.int32, gr.shape, 0) >> _log2(ML_HEADS)
    lfc = jnp.where((kind_c & 1) == 1, _log_sigmoid(gc), gc)
    lfr = jnp.where((kind_r & 1) == 1, _log_sigmoid(gr), gr)
    if chunk is None:
        gc_ref[...] = lfc
        gr_ref[...] = lfr
        return
    ri = lax.broadcasted_iota(jnp.int32, (tm, tm), 0)
    ci = lax.broadcasted_iota(jnp.int32, (tm, tm), 1)
    same = (ri >> _log2(chunk)) == (ci >> _log2(chunk))
    lower = jnp.where(same & (ci <= ri), 1.0, 0.0).astype(F32)
    upper = jnp.where(same & (ci >= ri), 1.0, 0.0).astype(F32)
    pre_c = _dot_exact(lower, lfc)
    suf_c = _dot_exact(upper, lfc)
    gc_ref[...] = jnp.where(kind_c == 1, pre_c, jnp.where(kind_c == 3, suf_c, gc))
    pre_r = _dot_exact(lfr, upper)
    suf_r = _dot_exact(lfr, lower)
    gr_ref[...] = jnp.where(kind_r == 1, pre_r, jnp.where(kind_r == 3, suf_r, gr))


def _proj(x2d, g_pre, wts, *, tm, chunk):
    rows = x2d.shape[0]
    assert rows % tm == 0
    wna, wqk, wvzo, wmg, wg, wgt, bna, bqk, bvzo, bmg, bg, bgt = wts
    const = lambda i: (0, 0)
    row = lambda i: (i, 0)

    def full(a):
        return pl.BlockSpec(a.shape, const)

    in_specs = [pl.BlockSpec((tm, D_MODEL), row), full(g_pre)] + [full(a) for a in wts]
    out_shape = (
        jax.ShapeDtypeStruct((rows, 4 * NA_WIDTH), BF16),
        jax.ShapeDtypeStruct((rows, 2 * ML_WIDTH), F32),
        jax.ShapeDtypeStruct((rows, 3 * ML_WIDTH), BF16),
        jax.ShapeDtypeStruct((rows, 2 * D_MODEL), BF16),
        jax.ShapeDtypeStruct((rows, N_GATES), F32),
        jax.ShapeDtypeStruct((N_GATES, rows), F32),
    )
    out_specs = (
        pl.BlockSpec((tm, 4 * NA_WIDTH), row),
        pl.BlockSpec((tm, 2 * ML_WIDTH), row),
        pl.BlockSpec((tm, 3 * ML_WIDTH), row),
        pl.BlockSpec((tm, 2 * D_MODEL), row),
        pl.BlockSpec((tm, N_GATES), row),
        pl.BlockSpec((N_GATES, tm), lambda i: (0, i)),
    )
    return pl.pallas_call(
        functools.partial(_proj_kernel, chunk=chunk),
        grid=(rows // tm,),
        in_specs=in_specs,
        out_specs=out_specs,
        out_shape=out_shape,
        compiler_params=_params(1),
        name="proj",
    )(x2d, g_pre, *wts)


def _na_row_start(r, n_rows):
    return jnp.clip(r - NA_WIN_ROWS // 2, 0, n_rows - NA_WIN_ROWS)


def _na_kernel(q_ref, k_ref, v_ref, z_ref, mk_ref, mv_ref, bias_ref, o_ref, *, n_rows):
    r = pl.program_id(1)
    r0 = _na_row_start(r, n_rows)
    start = pl.multiple_of(r0 * GRID_W, GRID_W)
    n_keys = NA_WIN_ROWS * GRID_W
    scale = NA_DH ** -0.5
    lane = lax.broadcasted_iota(jnp.int32, (GRID_W, NA_PAIR), 1)
    lo = lane < NA_DH
    n_pairs = NA_WIDTH // NA_PAIR
    groups = [slice(g * NA_PAIR, (g + 1) * NA_PAIR) for g in range(n_pairs)]

    scores = []
    for cols in groups:
        qf = q_ref[:, cols].astype(F32) * scale
        q2 = jnp.concatenate([jnp.where(lo, qf, 0.0), jnp.where(lo, 0.0, qf)], axis=0).astype(BF16)
        s = _dot_nt(q2, k_ref[pl.ds(start, n_keys), cols])
        sm = _dot_nt(q2, mk_ref[:, cols])
        scores.append((s, sm))
    probs = []
    for g, (s, sm) in enumerate(scores):
        bias = bias_ref[0, g]
        s = jnp.where(bias > 0.5 * NEG, s + bias, NEG)
        m = jnp.maximum(jnp.max(s, axis=-1, keepdims=True), jnp.max(sm, axis=-1, keepdims=True))
        p = jnp.exp(s - m)
        pm = jnp.exp(sm - m)
        denom = jnp.sum(p, axis=-1, keepdims=True) + jnp.sum(pm, axis=-1, keepdims=True)
        probs.append((p.astype(BF16), pm.astype(BF16), denom))
    for cols, (p, pm, denom) in zip(groups, probs):
        o2 = _dot(p, v_ref[pl.ds(start, n_keys), cols]) + _dot(pm, mv_ref[:, cols])
        o2 = o2 / denom
        o = jnp.where(lo, o2[0:GRID_W], o2[GRID_W:2 * GRID_W])
        o_ref[:, cols] = (o * _silu(z_ref[:, cols].astype(F32))).astype(BF16)


def _na_bias_table(rpb):
    c = np.arange(GRID_W)
    c0 = np.clip(c - NA_WIN_COLS // 2, 0, GRID_W - NA_WIN_COLS)
    kc = np.arange(GRID_W)
    valid = (kc[None, :] >= c0[:, None]) & (kc[None, :] < c0[:, None] + NA_WIN_COLS)
    dc = kc[None, :] - c[:, None] + NA_WIN_COLS - 1
    n_dc = 2 * NA_WIN_COLS - 1
    n_dr = 2 * NA_WIN_ROWS - 1
    onehot = ((dc[None] == np.arange(n_dc)[:, None, None]) & valid[None]).astype(np.float32)
    toe = jnp.dot(rpb.astype(F32).reshape(NA_HEADS * n_dr, n_dc), onehot.reshape(n_dc, GRID_W * GRID_W),
                  precision=lax.Precision.HIGHEST)
    toe = jnp.where(valid.reshape(1, GRID_W * GRID_W), toe, NEG)
    toe = toe.reshape(NA_HEADS, n_dr, GRID_W, GRID_W)
    tab = jnp.stack([toe[:, off:off + NA_WIN_ROWS] for off in range(NA_WIN_ROWS)])
    tab = jnp.transpose(tab, (0, 1, 3, 2, 4))
    return tab.reshape(NA_WIN_ROWS, NA_WIDTH // NA_PAIR, 2 * GRID_W, NA_WIN_ROWS * GRID_W)


def _na(na, meta_na, bias_tab, *, batch, seq):
    n_rows = seq // GRID_W
    assert n_rows >= NA_WIN_ROWS and seq % GRID_W == 0

    def bias_idx(b, r):
        return (_na_row_start(r, n_rows) - r + NA_WIN_ROWS - 1, 0, 0, 0)

    return pl.pallas_call(
        functools.partial(_na_kernel, n_rows=n_rows),
        grid=(batch, n_rows),
        in_specs=[
            pl.BlockSpec((GRID_W, NA_WIDTH), lambda b, r: (b * n_rows + r, 0)),
            pl.BlockSpec((seq, NA_WIDTH), lambda b, r: (b, 1)),
            pl.BlockSpec((seq, NA_WIDTH), lambda b, r: (b, 2)),
            pl.BlockSpec((GRID_W, NA_WIDTH), lambda b, r: (b * n_rows + r, 3)),
            pl.BlockSpec((N_META, NA_WIDTH), lambda b, r: (0, 1)),
            pl.BlockSpec((N_META, NA_WIDTH), lambda b, r: (0, 2)),
            pl.BlockSpec((1, NA_WIDTH // NA_PAIR, 2 * GRID_W, NA_WIN_ROWS * GRID_W), bias_idx),
        ],
        out_specs=pl.BlockSpec((GRID_W, NA_WIDTH), lambda b, r: (b * n_rows + r, 0)),
        out_shape=jax.ShapeDtypeStruct((batch * seq, NA_WIDTH), BF16),
        compiler_params=_params(2),
        name="na",
    )(na, na, na, na, meta_na, meta_na, bias_tab)


def _conv_silu(x_ref, prev8, next8, w, xs_ref):
    tb = x_ref.shape[0]
    xs_ref[0:8, :] = prev8
    xs_ref[8:tb + 8, :] = x_ref[...]
    xs_ref[tb + 8:tb + 16, :] = next8
    y = (w[0:1] * xs_ref[7:tb + 7, :] + w[1:2] * xs_ref[8:tb + 8, :]
         + w[2:3] * xs_ref[9:tb + 9, :])
    return _silu(y)


def _twice(row):
    return jnp.concatenate([row, row], axis=1)


def _ml_block(q_all, k_all, v_ref, gc_ref, gr_ref, ct_ref, m_ref, emit, *, reverse):
    tb = q_all.shape[0]
    ch = ML_CHUNK
    n_chunks = tb // ch
    order = range(n_chunks - 1, -1, -1) if reverse else range(n_chunks)
    ig_i, b_i = (2 * ML_HEADS, 3 * ML_HEADS) if reverse else (0, ML_HEADS)
    ones = jnp.ones((ch, ML_DH), BF16)
    ri = lax.broadcasted_iota(jnp.int32, (ch, ch), 0)
    ci = lax.broadcasted_iota(jnp.int32, (ch, ch), 1)
    causal = (ci >= ri) if reverse else (ci <= ri)
    heads = range(ML_HEADS)
    cts = [ct_ref[hd] for hd in heads]
    ms = [m_ref[hd][0:1, :] for hd in heads]
    for c in order:
        rows = slice(c * ch, (c + 1) * ch)
        pre = []
        for hd in heads:
            cols = slice(hd * ML_DH, (hd + 1) * ML_DH)
            bc = jnp.broadcast_to(gc_ref[rows, b_i + hd:b_i + hd + 1], (ch, ch))
            ic = jnp.broadcast_to(gc_ref[rows, ig_i + hd:ig_i + hd + 1], (ch, ch))
            rr = jnp.broadcast_to(gr_ref[ig_i + hd:ig_i + hd + 1, rows]
                                  - gr_ref[b_i + hd:b_i + hd + 1, rows], (ch, ch))
            g = bc[0:1] if reverse else bc[ch - 1:ch]
            logd = jnp.where(causal, bc + rr, NEG)
            rowmax = jnp.max(logd, axis=1, keepdims=True)
            a = g - bc + ic
            m_loc = jnp.max(a, axis=0, keepdims=True)
            k = k_all[rows, cols]
            q = q_all[rows, cols].astype(BF16)
            qk = _dot_nt(q, k.astype(BF16))
            v_ext = jnp.concatenate([v_ref[rows, cols], ones], axis=1)
            dct = _dot_tn((jnp.exp(a - m_loc) * k.astype(F32)).astype(BF16), v_ext)
            pre.append((cols, bc, g, logd, rowmax, m_loc, q, qk, v_ext, dct))
        inter_parts = [_dot(p[6], cts[hd].astype(BF16)) for hd, p in zip(heads, pre)]
        for hd, (cols, bc, g, logd, rowmax, m_loc, q, qk, v_ext, dct) in zip(heads, pre):
            m = ms[hd]
            inter = bc + m
            m_t = jnp.maximum(rowmax, inter)
            s = (qk * jnp.exp(logd - m_t)).astype(BF16)
            e_inter = jnp.exp(inter - m_t)
            r1 = _dot(s, v_ext)
            r2 = inter_parts[hd]
            num = r1[:, :ML_DH] + e_inter * r2[:, :ML_DH]
            den = r1[:, ML_DH:] + e_inter * r2[:, ML_DH:]
            emit(hd, rows, cols, num / jnp.maximum(jnp.abs(den), jnp.exp(-m_t)))
            m_new = jnp.maximum(g + m, m_loc)
            cts[hd] = _twice(jnp.exp(g + m - m_new)) * cts[hd] + _twice(jnp.exp(m_loc - m_new)) * dct
            ms[hd] = m_new
    for hd in heads:
        ct_ref[hd] = cts[hd]
        m_ref[hd] = jnp.broadcast_to(ms[hd], m_ref.shape[1:])


def _ml_fwd_kernel(qp_ref, kp_ref, qprev_ref, kprev_ref, qnext_ref, knext_ref, v_ref, gc_ref, gr_ref,
                   mqk_ref, mv_ref, mgc_ref, mgr_ref, cw_ref, hf_ref, qc_ref, kc_ref,
                   ct_ref, m_ref, xs_ref, *, n_blocks):
    i = pl.program_id(1)

    @pl.when(i == 0)
    def _():
        w = cw_ref[:, ML_WIDTH:2 * ML_WIDTH]
        xs_ref[0:8, :] = jnp.zeros((8, ML_WIDTH), F32)
        xs_ref[8:8 + N_META, :] = mqk_ref[:, ML_WIDTH:2 * ML_WIDTH]
        xs_ref[8 + N_META:16 + N_META, :] = kp_ref[0:8, :]
        y = (w[0:1] * xs_ref[7:7 + N_META, :] + w[1:2] * xs_ref[8:8 + N_META, :]
             + w[2:3] * xs_ref[9:9 + N_META, :])
        mk = _silu(y) * (ML_DH ** -0.5)
        ones = jnp.ones((N_META, ML_DH), BF16)
        ri = lax.broadcasted_iota(jnp.int32, (N_META, N_META), 0)
        ci = lax.broadcasted_iota(jnp.int32, (N_META, N_META), 1)
        for hd in range(ML_HEADS):
            cols = slice(hd * ML_DH, (hd + 1) * ML_DH)
            ig = mgc_ref[:, hd:hd + 1]
            lf = mgr_ref[ML_HEADS + hd:ML_HEADS + hd + 1, :]
            a = jnp.sum(jnp.where(ci > ri, lf, 0.0), axis=1, keepdims=True) + ig
            g = jnp.sum(lf, axis=1, keepdims=True)
            m_loc = jnp.max(a, axis=0, keepdims=True)
            m0 = jnp.maximum(g, m_loc)
            v_ext = jnp.concatenate([mv_ref[:, cols], ones], axis=1)
            dct = _dot_tn((jnp.exp(a - m_loc) * mk[:, cols]).astype(BF16), v_ext)
            ct_ref[hd] = jnp.exp(m_loc - m0) * dct
            m_ref[hd] = jnp.broadcast_to(m0, m_ref.shape[1:])

    first = i == 0
    last = i == n_blocks - 1
    zeros8 = jnp.zeros((8, ML_WIDTH), F32)
    qprev = jnp.where(first, mqk_ref[8:16, 0:ML_WIDTH], qprev_ref[...])
    kprev = jnp.where(first, mqk_ref[8:16, ML_WIDTH:2 * ML_WIDTH], kprev_ref[...])
    qnext = jnp.where(last, zeros8, qnext_ref[...])
    knext = jnp.where(last, zeros8, knext_ref[...])

    def emit(hd, rows, cols, h):
        hf_ref[rows, cols] = h

    q_all = _conv_silu(qp_ref, qprev, qnext, cw_ref[:, 0:ML_WIDTH], xs_ref).astype(BF16)
    k_all = _conv_silu(kp_ref, kprev, knext, cw_ref[:, ML_WIDTH:2 * ML_WIDTH], xs_ref) * (ML_DH ** -0.5)
    qc_ref[...] = q_all
    kc_ref[...] = k_all.astype(BF16)
    _ml_block(q_all, k_all, v_ref, gc_ref, gr_ref, ct_ref, m_ref, emit, reverse=False)


def _ml_bwd_kernel(qc_ref, kc_ref, v_ref, gc_ref, gr_ref, hf_ref, o_ref, z_ref, hg_ref, y_ref,
                   ct_ref, m_ref):
    @pl.when(pl.program_id(1) == 0)
    def _():
        ct_ref[...] = jnp.zeros(ct_ref.shape, F32)
        m_ref[...] = jnp.zeros(m_ref.shape, F32)

    def emit(hd, rows, cols, h):
        hh = jax.nn.sigmoid(o_ref[rows, cols].astype(F32)) * (hf_ref[rows, cols] + h)
        mu = jnp.mean(hh, axis=-1, keepdims=True)
        d = hh - mu
        var = jnp.mean(d * d, axis=-1, keepdims=True)
        y = d * lax.rsqrt(var + EPS) * hg_ref[:, cols] * _silu(z_ref[rows, cols].astype(F32))
        y_ref[rows, cols] = y.astype(BF16)

    _ml_block(qc_ref[...], kc_ref[...], v_ref, gc_ref, gr_ref, ct_ref, m_ref, emit, reverse=True)


def _ml_specs(batch, seq, tb, blk):
    nb = seq // tb
    n8 = tb // 8
    total8 = batch * seq // 8

    def main(col):
        return pl.BlockSpec((tb, ML_WIDTH), lambda b, j: (b * nb + blk(b, j), col))

    def prev(col):
        return pl.BlockSpec((8, ML_WIDTH), lambda b, j: (jnp.maximum((b * nb + blk(b, j)) * n8 - 1, 0), col))

    def nxt(col):
        return pl.BlockSpec((8, ML_WIDTH),
                            lambda b, j: (jnp.minimum((b * nb + blk(b, j) + 1) * n8, total8 - 1), col))

    gc = pl.BlockSpec((tb, N_GATES), lambda b, j: (b * nb + blk(b, j), 0))
    gr = pl.BlockSpec((N_GATES, tb), lambda b, j: (0, b * nb + blk(b, j)))
    return main, prev, nxt, gc, gr


def _ml_state_scratch():
    return [pltpu.VMEM((ML_HEADS, ML_DH, 2 * ML_DH), F32), pltpu.VMEM((ML_HEADS, 8, 128), F32)]


def _ml_fwd(qk, vzo, gc, gr, meta_qk, meta_vzo, meta_gc, meta_gr, conv_w, *, batch, seq, tb):
    nb = seq // tb
    main, prev, nxt, gc_spec, gr_spec = _ml_specs(batch, seq, tb, lambda b, j: j)
    const = lambda b, j: (0, 0)
    rows = batch * seq
    return pl.pallas_call(
        functools.partial(_ml_fwd_kernel, n_blocks=nb),
        grid=(batch, nb),
        in_specs=[main(0), main(1), prev(0), prev(1), nxt(0), nxt(1), main(0), gc_spec, gr_spec,
                  pl.BlockSpec(meta_qk.shape, const),
                  pl.BlockSpec((N_META, ML_WIDTH), const),
                  pl.BlockSpec(meta_gc.shape, const),
                  pl.BlockSpec(meta_gr.shape, const),
                  pl.BlockSpec(conv_w.shape, const)],
        out_specs=(main(0), main(0), main(0)),
        out_shape=(jax.ShapeDtypeStruct((rows, ML_WIDTH), F32),
                   jax.ShapeDtypeStruct((rows, ML_WIDTH), BF16),
                   jax.ShapeDtypeStruct((rows, ML_WIDTH), BF16)),
        scratch_shapes=_ml_state_scratch() + [pltpu.VMEM((tb + 16, ML_WIDTH), F32)],
        compiler_params=_params(2),
        name="ml_fwd",
    )(qk, qk, qk, qk, qk, qk, vzo, gc, gr, meta_qk, meta_vzo, meta_gc, meta_gr, conv_w)


def _ml_bwd(qc, kc, vzo, gc, gr, hf, head_g, *, batch, seq, tb):
    nb = seq // tb
    main, _, _, gc_spec, gr_spec = _ml_specs(batch, seq, tb, lambda b, j: nb - 1 - j)
    const = lambda b, j: (0, 0)
    return pl.pallas_call(
        _ml_bwd_kernel,
        grid=(batch, nb),
        in_specs=[main(0), main(0), main(0), gc_spec, gr_spec, main(0), main(2), main(1),
                  pl.BlockSpec(head_g.shape, const)],
        out_specs=main(0),
        out_shape=jax.ShapeDtypeStruct((batch * seq, ML_WIDTH), BF16),
        scratch_shapes=_ml_state_scratch(),
        compiler_params=_params(2),
        name="ml_bwd",
    )(qc, kc, vzo, gc, gr, hf, vzo, vzo, head_g)


def _out_kernel(x_ref, ya_ref, yb_ref, mg_ref, wa_ref, wb_ref, wo_ref, g_ref, o_ref):
    ya = _dot(ya_ref[...], wa_ref[...])
    yb = _dot(yb_ref[...], wb_ref[...])
    ga = jax.nn.sigmoid(mg_ref[:, 0:D_MODEL].astype(F32))
    gb = jax.nn.sigmoid(mg_ref[:, D_MODEL:2 * D_MODEL].astype(F32))
    y = _dot((ga * ya + gb * yb).astype(BF16), wo_ref[...])
    yn = y * lax.rsqrt(jnp.mean(y * y, axis=-1, keepdims=True) + EPS) * g_ref[...]
    o_ref[...] = x_ref[...] + yn


def _out(x2d, ya, yb, mg, w_a, w_b, w_out, g_post, *, tm):
    rows = x2d.shape[0]
    row = lambda i: (i, 0)
    const = lambda i: (0, 0)
    return pl.pallas_call(
        _out_kernel,
        grid=(rows // tm,),
        in_specs=[pl.BlockSpec((tm, D_MODEL), row),
                  pl.BlockSpec((tm, NA_WIDTH), row),
                  pl.BlockSpec((tm, ML_WIDTH), row),
                  pl.BlockSpec((tm, 2 * D_MODEL), row),
                  pl.BlockSpec(w_a.shape, const),
                  pl.BlockSpec(w_b.shape, const),
                  pl.BlockSpec(w_out.shape, const),
                  pl.BlockSpec(g_post.shape, const)],
        out_specs=pl.BlockSpec((tm, D_MODEL), row),
        out_shape=jax.ShapeDtypeStruct((rows, D_MODEL), F32),
        compiler_params=_params(1),
        name="out",
    )(x2d, ya, yb, mg, w_a, w_b, w_out, g_post)


def _split_in_proj(w_in, b_in):
    o_qk = 4 * NA_WIDTH
    o_vzo = o_qk + 2 * ML_WIDTH
    o_g = o_vzo + 3 * ML_WIDTH
    o_mg = o_g + N_GATES
    wb = w_in.astype(BF16)
    b = b_in.astype(F32)[None, :]
    wg = wb[:, o_g:o_mg]
    bg = b[:, o_g:o_mg]
    return (wb[:, :o_qk], wb[:, o_qk:o_vzo], wb[:, o_vzo:o_g], wb[:, o_mg:], wg, wg.T,
            b[:, :o_qk], b[:, o_qk:o_vzo], b[:, o_vzo:o_g], b[:, o_mg:], bg, bg.T)


def _trunk(x, meta_proj, wts, bias_tab, g_pre, conv_w, head_g, w_a, w_b, w_out, g_post):
    batch, seq, _ = x.shape
    assert seq % ML_BLOCK == 0 and seq % PROJ_ROWS == 0 and ML_BLOCK % ML_CHUNK == 0
    assert PROJ_ROWS % ML_CHUNK == 0
    x2d = x.reshape(batch * seq, D_MODEL)
    m_na, m_qk, m_vzo, _, m_gc, m_gr = meta_proj
    na, qk, vzo, mg, gc, gr = _proj(x2d, g_pre, wts, tm=PROJ_ROWS, chunk=ML_CHUNK)
    ya = _na(na, m_na, bias_tab, batch=batch, seq=seq)
    hf, qc, kc = _ml_fwd(qk, vzo, gc, gr, m_qk, m_vzo, m_gc, m_gr, conv_w, batch=batch, seq=seq, tb=ML_BLOCK)
    yb = _ml_bwd(qc, kc, vzo, gc, gr, hf, head_g, batch=batch, seq=seq, tb=ML_BLOCK)
    y = _out(x2d, ya, yb, mg, w_a, w_b, w_out, g_post, tm=PROJ_ROWS)
    return y.reshape(batch, seq, D_MODEL)


def kernel(x_prompt, x_sample, meta_tokens, g_pre, w_in, b_in, na_rpb, ml_conv_w, ml_head_g, w_a, w_b, w_out, g_post):
    assert w_in.shape[0] == 1, "the meta-token outputs are only droppable for a single layer"
    wts = _split_in_proj(w_in[0], b_in[0])
    gp = g_pre[0].astype(F32)[None, :]
    meta_proj = _proj(meta_tokens.astype(F32), gp, wts, tm=N_META, chunk=None)
    bias_tab = _na_bias_table(na_rpb[0])
    args = (meta_proj, wts, bias_tab, gp, ml_conv_w[0].astype(F32), ml_head_g[0].astype(F32)[None, :],
            w_a[0].astype(BF16), w_b[0].astype(BF16), w_out[0].astype(BF16), g_post[0].astype(F32)[None, :])
    return (_trunk(x_prompt, *args), _trunk(x_sample, *args))
```

```python
import functools

import jax
import jax.numpy as jnp
import numpy as np
from jax import lax
from jax.experimental import pallas as pl
from jax.experimental.pallas import tpu as pltpu

D_MODEL = 1024
N_META = 16
GRID_W = 64
EPS = 1e-6
NEG = -1e30
NA_HEADS = 8
NA_DH = 64
NA_WIDTH = NA_HEADS * NA_DH
NA_PAIR = 2 * NA_DH
NA_WIN_ROWS = 8
NA_WIN_COLS = 16
NA_ROWS_PER_STEP = 4
ML_HEADS = 4
ML_DH = 128
ML_WIDTH = ML_HEADS * ML_DH
N_GATES = 4 * ML_HEADS
ML_CHUNK = 128
ML_BLOCK = 256
PROJ_ROWS = 512
VMEM_LIMIT_BYTES = 56 * 1024 * 1024

F32 = jnp.float32
BF16 = jnp.bfloat16


def _dot(a, b):
    return jnp.dot(a, b, preferred_element_type=F32)


def _dot_nt(a, b):
    return lax.dot_general(a, b, (((1,), (1,)), ((), ())), preferred_element_type=F32)


def _dot_tn(a, b):
    return lax.dot_general(a, b, (((0,), (0,)), ((), ())), preferred_element_type=F32)


def _split3(x):
    hi = x.astype(BF16)
    r1 = x - hi.astype(F32)
    mid = r1.astype(BF16)
    lo = (r1 - mid.astype(F32)).astype(BF16)
    return hi, mid, lo


def _silu(x):
    return x * jax.nn.sigmoid(x)


def _log_sigmoid(x):
    return jnp.minimum(x, 0.0) - jnp.log1p(jnp.exp(-jnp.abs(x)))


def _resident(a):
    return pl.BlockSpec(a.shape, lambda *_: (0,) * a.ndim, pipeline_mode=pl.Buffered(1))


def _log2(n):
    assert n & (n - 1) == 0
    return n.bit_length() - 1


def _params(n_grid_axes):
    return pltpu.CompilerParams(dimension_semantics=("arbitrary",) * n_grid_axes,
                                vmem_limit_bytes=VMEM_LIMIT_BYTES)


def _proj_kernel(x_ref, g_ref, wna_ref, wqk_ref, wvzo_ref, wmg_ref, wg_ref, wgt_ref,
                 bna_ref, bqk_ref, bvzo_ref, bmg_ref, bg_ref, bgt_ref,
                 na_ref, qk_ref, vzo_ref, mg_ref, gc_ref, gr_ref, *, chunk):
    x = x_ref[...]
    xn = x * lax.rsqrt(jnp.mean(x * x, axis=-1, keepdims=True) + EPS) * g_ref[...]
    xb = xn.astype(BF16)
    na_ref[...] = (_dot(xb, wna_ref[...]) + bna_ref[...]).astype(BF16)
    qk_ref[...] = _dot(xb, wqk_ref[...]) + bqk_ref[...]
    vzo_ref[...] = (_dot(xb, wvzo_ref[...]) + bvzo_ref[...]).astype(BF16)
    mg_ref[...] = (_dot(xb, wmg_ref[...]) + bmg_ref[...]).astype(BF16)

    tm = x.shape[0]
    gc = _dot(xb, wg_ref[...]) + bg_ref[...]
    gr = _dot_nt(wgt_ref[...], xb) + bgt_ref[...]
    kind_c = lax.broadcasted_iota(jnp.int32, gc.shape, 1) >> _log2(ML_HEADS)
    kind_r = lax.broadcasted_iota(jnp.int32, gr.shape, 0) >> _log2(ML_HEADS)
    lfc = jnp.where((kind_c & 1) == 1, _log_sigmoid(gc), gc)
    lfr = jnp.where((kind_r & 1) == 1, _log_sigmoid(gr), gr)
    if chunk is None:
        gc_ref[...] = lfc
        gr_ref[...] = lfr
        return
    ri = lax.broadcasted_iota(jnp.int32, (chunk, chunk), 0)
    ci = lax.broadcasted_iota(jnp.int32, (chunk, chunk), 1)
    lower = jnp.where(ci <= ri, 1.0, 0.0).astype(BF16)
    upper = jnp.where(ci >= ri, 1.0, 0.0).astype(BF16)
    both = jnp.concatenate([lower, upper], axis=0)
    kc = lax.broadcasted_iota(jnp.int32, (chunk, N_GATES), 1) >> _log2(ML_HEADS)
    kr = lax.broadcasted_iota(jnp.int32, (N_GATES, chunk), 0) >> _log2(ML_HEADS)
    for c in range(tm // chunk):
        rows = slice(c * chunk, (c + 1) * chunk)
        pieces = _split3(lfr[:, rows])
        cs = sum(_dot_nt(both, piece) for piece in pieces)
        gc_ref[rows, :] = jnp.where(kc == 1, cs[:chunk], jnp.where(kc == 3, cs[chunk:], gc[rows]))
        pre_r = sum(_dot(piece, upper) for piece in pieces)
        suf_r = sum(_dot(piece, lower) for piece in pieces)
        gr_ref[:, rows] = jnp.where(kr == 1, pre_r, jnp.where(kr == 3, suf_r, gr[:, rows]))


def _proj(x2d, g_pre, wts, *, tm, chunk):
    rows = x2d.shape[0]
    assert rows % tm == 0
    row = lambda i: (i, 0)
    in_specs = [pl.BlockSpec((tm, D_MODEL), row), _resident(g_pre)] + [_resident(a) for a in wts]
    out_shape = (
        jax.ShapeDtypeStruct((rows, 4 * NA_WIDTH), BF16),
        jax.ShapeDtypeStruct((rows, 2 * ML_WIDTH), F32),
        jax.ShapeDtypeStruct((rows, 3 * ML_WIDTH), BF16),
        jax.ShapeDtypeStruct((rows, 2 * D_MODEL), BF16),
        jax.ShapeDtypeStruct((rows, N_GATES), F32),
        jax.ShapeDtypeStruct((N_GATES, rows), F32),
    )
    out_specs = (
        pl.BlockSpec((tm, 4 * NA_WIDTH), row),
        pl.BlockSpec((tm, 2 * ML_WIDTH), row),
        pl.BlockSpec((tm, 3 * ML_WIDTH), row),
        pl.BlockSpec((tm, 2 * D_MODEL), row),
        pl.BlockSpec((tm, N_GATES), row),
        pl.BlockSpec((N_GATES, tm), lambda i: (0, i)),
    )
    return pl.pallas_call(
        functools.partial(_proj_kernel, chunk=chunk),
        grid=(rows // tm,),
        in_specs=in_specs,
        out_specs=out_specs,
        out_shape=out_shape,
        compiler_params=_params(1),
        name="proj",
    )(x2d, g_pre, *wts)


def _na_row_start(r, n_rows):
    return jnp.clip(r - NA_WIN_ROWS // 2, 0, n_rows - NA_WIN_ROWS)


def _na_kernel(q_ref, k_ref, v_ref, z_ref, mk_ref, mv_ref, bias_ref, o_ref, *, n_rows, rows_per_step):
    n_keys = NA_WIN_ROWS * GRID_W
    scale = NA_DH ** -0.5
    lane = lax.broadcasted_iota(jnp.int32, (GRID_W, NA_PAIR), 1)
    lo = lane < NA_DH
    n_pairs = NA_WIDTH // NA_PAIR
    groups = [slice(g * NA_PAIR, (g + 1) * NA_PAIR) for g in range(n_pairs)]
    for i in range(rows_per_step):
        r = pl.program_id(1) * rows_per_step + i
        r0 = _na_row_start(r, n_rows)
        start = pl.multiple_of(r0 * GRID_W, GRID_W)
        off = r0 - r + NA_WIN_ROWS - 1
        qrows = slice(i * GRID_W, (i + 1) * GRID_W)
        scores = []
        for cols in groups:
            qf = q_ref[qrows, cols].astype(F32) * scale
            q2 = jnp.concatenate([jnp.where(lo, qf, 0.0), jnp.where(lo, 0.0, qf)], axis=0).astype(BF16)
            s = _dot_nt(q2, k_ref[pl.ds(start, n_keys), cols])
            sm = _dot_nt(q2, mk_ref[:, cols])
            scores.append((s, sm))
        probs = []
        for g, (s, sm) in enumerate(scores):
            bias = bias_ref[off, g]
            s = jnp.where(bias > 0.5 * NEG, s + bias, NEG)
            m = jnp.maximum(jnp.max(s, axis=-1, keepdims=True), jnp.max(sm, axis=-1, keepdims=True))
            p = jnp.exp(s - m)
            pm = jnp.exp(sm - m)
            denom = jnp.sum(p, axis=-1, keepdims=True) + jnp.sum(pm, axis=-1, keepdims=True)
            probs.append((p.astype(BF16), pm.astype(BF16), denom))
        for cols, (p, pm, denom) in zip(groups, probs):
            o2 = _dot(p, v_ref[pl.ds(start, n_keys), cols]) + _dot(pm, mv_ref[:, cols])
            o2 = o2 / denom
            o = jnp.where(lo, o2[0:GRID_W], o2[GRID_W:2 * GRID_W])
            o_ref[qrows, cols] = (o * _silu(z_ref[qrows, cols].astype(F32))).astype(BF16)


def _na_bias_table(rpb):
    c = np.arange(GRID_W)
    c0 = np.clip(c - NA_WIN_COLS // 2, 0, GRID_W - NA_WIN_COLS)
    kc = np.arange(GRID_W)
    valid = (kc[None, :] >= c0[:, None]) & (kc[None, :] < c0[:, None] + NA_WIN_COLS)
    dc = kc[None, :] - c[:, None] + NA_WIN_COLS - 1
    n_dc = 2 * NA_WIN_COLS - 1
    n_dr = 2 * NA_WIN_ROWS - 1
    onehot = ((dc[None] == np.arange(n_dc)[:, None, None]) & valid[None]).astype(np.float32)
    toe = jnp.dot(rpb.astype(F32).reshape(NA_HEADS * n_dr, n_dc), onehot.reshape(n_dc, GRID_W * GRID_W),
                  precision=lax.Precision.HIGHEST)
    toe = jnp.where(valid.reshape(1, GRID_W * GRID_W), toe, NEG)
    toe = toe.reshape(NA_HEADS, n_dr, GRID_W, GRID_W)
    tab = jnp.stack([toe[:, off:off + NA_WIN_ROWS] for off in range(NA_WIN_ROWS)])
    tab = jnp.transpose(tab, (0, 1, 3, 2, 4))
    return tab.reshape(NA_WIN_ROWS, NA_WIDTH // NA_PAIR, 2 * GRID_W, NA_WIN_ROWS * GRID_W)


def _na(na, meta_na, bias_tab, *, batch, seq):
    n_rows = seq // GRID_W
    rps = NA_ROWS_PER_STEP
    assert n_rows >= NA_WIN_ROWS and seq % GRID_W == 0 and n_rows % rps == 0
    steps = n_rows // rps
    qz = lambda col: pl.BlockSpec((rps * GRID_W, NA_WIDTH), lambda b, r: (b * steps + r, col))
    kv = lambda col: pl.BlockSpec((seq, NA_WIDTH), lambda b, r: (b, col), pipeline_mode=pl.Buffered(1))
    meta = lambda col: pl.BlockSpec((N_META, NA_WIDTH), lambda b, r: (0, col), pipeline_mode=pl.Buffered(1))
    return pl.pallas_call(
        functools.partial(_na_kernel, n_rows=n_rows, rows_per_step=rps),
        grid=(batch, steps),
        in_specs=[qz(0), kv(1), kv(2), qz(3), meta(1), meta(2), _resident(bias_tab)],
        out_specs=qz(0),
        out_shape=jax.ShapeDtypeStruct((batch * seq, NA_WIDTH), BF16),
        compiler_params=_params(2),
        name="na",
    )(na, na, na, na, meta_na, meta_na, bias_tab)


def _conv_silu(x_ref, prev8, next8, w, xs_ref):
    tb = x_ref.shape[0]
    xs_ref[0:8, :] = prev8
    xs_ref[8:tb + 8, :] = x_ref[...]
    xs_ref[tb + 8:tb + 16, :] = next8
    y = (w[0:1] * xs_ref[7:tb + 7, :] + w[1:2] * xs_ref[8:tb + 8, :]
         + w[2:3] * xs_ref[9:tb + 9, :])
    return _silu(y)


def _twice(row):
    return jnp.concatenate([row, row], axis=1)


def _ml_block(q_all, k_all, v_ref, gc_ref, gr_ref, ct_ref, m_ref, emit, *, reverse):
    tb = q_all.shape[0]
    ch = ML_CHUNK
    n_chunks = tb // ch
    order = range(n_chunks - 1, -1, -1) if reverse else range(n_chunks)
    ig_i, b_i = (2 * ML_HEADS, 3 * ML_HEADS) if reverse else (0, ML_HEADS)
    ones = jnp.ones((ch, ML_DH), BF16)
    ri = lax.broadcasted_iota(jnp.int32, (ch, ch), 0)
    ci = lax.broadcasted_iota(jnp.int32, (ch, ch), 1)
    causal = (ci >= ri) if reverse else (ci <= ri)
    heads = range(ML_HEADS)
    cts = [ct_ref[hd] for hd in heads]
    ms = [m_ref[hd][0:1, :] for hd in heads]
    for c in order:
        rows = slice(c * ch, (c + 1) * ch)
        pre = []
        for hd in heads:
            cols = slice(hd * ML_DH, (hd + 1) * ML_DH)
            bc = jnp.broadcast_to(gc_ref[rows, b_i + hd:b_i + hd + 1], (ch, ch))
            ic = jnp.broadcast_to(gc_ref[rows, ig_i + hd:ig_i + hd + 1], (ch, ch))
            rr = jnp.broadcast_to(gr_ref[ig_i + hd:ig_i + hd + 1, rows]
                                  - gr_ref[b_i + hd:b_i + hd + 1, rows], (ch, ch))
            g = bc[0:1] if reverse else bc[ch - 1:ch]
            logd = jnp.where(causal, bc + rr, NEG)
            rowmax = jnp.max(logd, axis=1, keepdims=True)
            a = g - bc + ic
            m_loc = jnp.max(a, axis=0, keepdims=True)
            k = k_all[rows, cols]
            q = q_all[rows, cols].astype(BF16)
            qk = _dot_nt(q, k.astype(BF16))
            v_ext = jnp.concatenate([v_ref[rows, cols], ones], axis=1)
            dct = _dot_tn((jnp.exp(a - m_loc) * k.astype(F32)).astype(BF16), v_ext)
            pre.append((cols, bc, g, logd, rowmax, m_loc, q, qk, v_ext, dct))
        inter_parts = [_dot(p[6], cts[hd].astype(BF16)) for hd, p in zip(heads, pre)]
        for hd, (cols, bc, g, logd, rowmax, m_loc, q, qk, v_ext, dct) in zip(heads, pre):
            m = ms[hd]
            inter = bc + m
            m_t = jnp.maximum(rowmax, inter)
            s = (qk * jnp.exp(logd - m_t)).astype(BF16)
            e_inter = jnp.exp(inter - m_t)
            r1 = _dot(s, v_ext)
            r2 = inter_parts[hd]
            num = r1[:, :ML_DH] + e_inter * r2[:, :ML_DH]
            den = r1[:, ML_DH:] + e_inter * r2[:, ML_DH:]
            emit(hd, rows, cols, num / jnp.maximum(jnp.abs(den), jnp.exp(-m_t)))
            m_new = jnp.maximum(g + m, m_loc)
            cts[hd] = _twice(jnp.exp(g + m - m_new)) * cts[hd] + _twice(jnp.exp(m_loc - m_new)) * dct
            ms[hd] = m_new
    for hd in heads:
        ct_ref[hd] = cts[hd]
        m_ref[hd] = jnp.broadcast_to(ms[hd], m_ref.shape[1:])


def _ml_fwd_kernel(qp_ref, kp_ref, qprev_ref, kprev_ref, qnext_ref, knext_ref, v_ref, gc_ref, gr_ref,
                   mqk_ref, mv_ref, mgc_ref, mgr_ref, cw_ref, hf_ref, qc_ref, kc_ref,
                   ct_ref, m_ref, xs_ref, *, n_blocks):
    i = pl.program_id(1)

    @pl.when(i == 0)
    def _():
        w = cw_ref[:, ML_WIDTH:2 * ML_WIDTH]
        xs_ref[0:8, :] = jnp.zeros((8, ML_WIDTH), F32)
        xs_ref[8:8 + N_META, :] = mqk_ref[:, ML_WIDTH:2 * ML_WIDTH]
        xs_ref[8 + N_META:16 + N_META, :] = kp_ref[0:8, :]
        y = (w[0:1] * xs_ref[7:7 + N_META, :] + w[1:2] * xs_ref[8:8 + N_META, :]
             + w[2:3] * xs_ref[9:9 + N_META, :])
        mk = _silu(y) * (ML_DH ** -0.5)
        ones = jnp.ones((N_META, ML_DH), BF16)
        ri = lax.broadcasted_iota(jnp.int32, (N_META, N_META), 0)
        ci = lax.broadcasted_iota(jnp.int32, (N_META, N_META), 1)
        for hd in range(ML_HEADS):
            cols = slice(hd * ML_DH, (hd + 1) * ML_DH)
            ig = mgc_ref[:, hd:hd + 1]
            lf = mgr_ref[ML_HEADS + hd:ML_HEADS + hd + 1, :]
            a = jnp.sum(jnp.where(ci > ri, lf, 0.0), axis=1, keepdims=True) + ig
            g = jnp.sum(lf, axis=1, keepdims=True)
            m_loc = jnp.max(a, axis=0, keepdims=True)
            m0 = jnp.maximum(g, m_loc)
            v_ext = jnp.concatenate([mv_ref[:, cols], ones], axis=1)
            dct = _dot_tn((jnp.exp(a - m_loc) * mk[:, cols]).astype(BF16), v_ext)
            ct_ref[hd] = jnp.exp(m_loc - m0) * dct
            m_ref[hd] = jnp.broadcast_to(m0, m_ref.shape[1:])

    first = i == 0
    last = i == n_blocks - 1
    zeros8 = jnp.zeros((8, ML_WIDTH), F32)
    qprev = jnp.where(first, mqk_ref[8:16, 0:ML_WIDTH], qprev_ref[...])
    kprev = jnp.where(first, mqk_ref[8:16, ML_WIDTH:2 * ML_WIDTH], kprev_ref[...])
    qnext = jnp.where(last, zeros8, qnext_ref[...])
    knext = jnp.where(last, zeros8, knext_ref[...])

    def emit(hd, rows, cols, h):
        hf_ref[rows, cols] = h

    q_all = _conv_silu(qp_ref, qprev, qnext, cw_ref[:, 0:ML_WIDTH], xs_ref).astype(BF16)
    k_all = _conv_silu(kp_ref, kprev, knext, cw_ref[:, ML_WIDTH:2 * ML_WIDTH], xs_ref) * (ML_DH ** -0.5)
    qc_ref[...] = q_all
    kc_ref[...] = k_all.astype(BF16)
    _ml_block(q_all, k_all, v_ref, gc_ref, gr_ref, ct_ref, m_ref, emit, reverse=False)


def _ml_bwd_kernel(qc_ref, kc_ref, v_ref, gc_ref, gr_ref, hf_ref, o_ref, z_ref, hg_ref, y_ref,
                   ct_ref, m_ref):
    @pl.when(pl.program_id(1) == 0)
    def _():
        ct_ref[...] = jnp.zeros(ct_ref.shape, F32)
        m_ref[...] = jnp.zeros(m_ref.shape, F32)

    def emit(hd, rows, cols, h):
        hh = jax.nn.sigmoid(o_ref[rows, cols].astype(F32)) * (hf_ref[rows, cols] + h)
        mu = jnp.mean(hh, axis=-1, keepdims=True)
        d = hh - mu
        var = jnp.mean(d * d, axis=-1, keepdims=True)
        y = d * lax.rsqrt(var + EPS) * hg_ref[:, cols] * _silu(z_ref[rows, cols].astype(F32))
        y_ref[rows, cols] = y.astype(BF16)

    _ml_block(qc_ref[...], kc_ref[...], v_ref, gc_ref, gr_ref, ct_ref, m_ref, emit, reverse=True)


def _ml_specs(batch, seq, tb, blk):
    nb = seq // tb
    n8 = tb // 8
    total8 = batch * seq // 8

    def main(col):
        return pl.BlockSpec((tb, ML_WIDTH), lambda b, j: (b * nb + blk(b, j), col))

    def prev(col):
        return pl.BlockSpec((8, ML_WIDTH), lambda b, j: (jnp.maximum((b * nb + blk(b, j)) * n8 - 1, 0), col))

    def nxt(col):
        return pl.BlockSpec((8, ML_WIDTH),
                            lambda b, j: (jnp.minimum((b * nb + blk(b, j) + 1) * n8, total8 - 1), col))

    gc = pl.BlockSpec((tb, N_GATES), lambda b, j: (b * nb + blk(b, j), 0))
    gr = pl.BlockSpec((N_GATES, tb), lambda b, j: (0, b * nb + blk(b, j)))
    return main, prev, nxt, gc, gr


def _ml_state_scratch():
    return [pltpu.VMEM((ML_HEADS, ML_DH, 2 * ML_DH), F32), pltpu.VMEM((ML_HEADS, 8, 128), F32)]


def _ml_fwd(qk, vzo, gc, gr, meta_qk, meta_vzo, meta_gc, meta_gr, conv_w, *, batch, seq, tb):
    nb = seq // tb
    main, prev, nxt, gc_spec, gr_spec = _ml_specs(batch, seq, tb, lambda b, j: j)
    const = lambda b, j: (0, 0)
    rows = batch * seq
    return pl.pallas_call(
        functools.partial(_ml_fwd_kernel, n_blocks=nb),
        grid=(batch, nb),
        in_specs=[main(0), main(1), prev(0), prev(1), nxt(0), nxt(1), main(0), gc_spec, gr_spec,
                  pl.BlockSpec(meta_qk.shape, const),
                  pl.BlockSpec((N_META, ML_WIDTH), const),
                  pl.BlockSpec(meta_gc.shape, const),
                  pl.BlockSpec(meta_gr.shape, const),
                  pl.BlockSpec(conv_w.shape, const)],
        out_specs=(main(0), main(0), main(0)),
        out_shape=(jax.ShapeDtypeStruct((rows, ML_WIDTH), F32),
                   jax.ShapeDtypeStruct((rows, ML_WIDTH), BF16),
                   jax.ShapeDtypeStruct((rows, ML_WIDTH), BF16)),
        scratch_shapes=_ml_state_scratch() + [pltpu.VMEM((tb + 16, ML_WIDTH), F32)],
        compiler_params=_params(2),
        name="ml_fwd",
    )(qk, qk, qk, qk, qk, qk, vzo, gc, gr, meta_qk, meta_vzo, meta_gc, meta_gr, conv_w)


def _ml_bwd(qc, kc, vzo, gc, gr, hf, head_g, *, batch, seq, tb):
    nb = seq // tb
    main, _, _, gc_spec, gr_spec = _ml_specs(batch, seq, tb, lambda b, j: nb - 1 - j)
    const = lambda b, j: (0, 0)
    return pl.pallas_call(
        _ml_bwd_kernel,
        grid=(batch, nb),
        in_specs=[main(0), main(0), main(0), gc_spec, gr_spec, main(0), main(2), main(1),
                  pl.BlockSpec(head_g.shape, const)],
        out_specs=main(0),
        out_shape=jax.ShapeDtypeStruct((batch * seq, ML_WIDTH), BF16),
        scratch_shapes=_ml_state_scratch(),
        compiler_params=_params(2),
        name="ml_bwd",
    )(qc, kc, vzo, gc, gr, hf, vzo, vzo, head_g)


def _out_kernel(x_ref, ya_ref, yb_ref, mg_ref, wa_ref, wb_ref, wo_ref, g_ref, o_ref):
    ya = _dot(ya_ref[...], wa_ref[...])
    yb = _dot(yb_ref[...], wb_ref[...])
    ga = jax.nn.sigmoid(mg_ref[:, 0:D_MODEL].astype(F32))
    gb = jax.nn.sigmoid(mg_ref[:, D_MODEL:2 * D_MODEL].astype(F32))
    y = _dot((ga * ya + gb * yb).astype(BF16), wo_ref[...])
    yn = y * lax.rsqrt(jnp.mean(y * y, axis=-1, keepdims=True) + EPS) * g_ref[...]
    o_ref[...] = x_ref[...] + yn


def _out(x2d, ya, yb, mg, w_a, w_b, w_out, g_post, *, tm):
    rows = x2d.shape[0]
    row = lambda i: (i, 0)
    return pl.pallas_call(
        _out_kernel,
        grid=(rows // tm,),
        in_specs=[pl.BlockSpec((tm, D_MODEL), row),
                  pl.BlockSpec((tm, NA_WIDTH), row),
                  pl.BlockSpec((tm, ML_WIDTH), row),
                  pl.BlockSpec((tm, 2 * D_MODEL), row),
                  _resident(w_a), _resident(w_b), _resident(w_out), _resident(g_post)],
        out_specs=pl.BlockSpec((tm, D_MODEL), row),
        out_shape=jax.ShapeDtypeStruct((rows, D_MODEL), F32),
        compiler_params=_params(1),
        name="out",
    )(x2d, ya, yb, mg, w_a, w_b, w_out, g_post)


def _split_in_proj(w_in, b_in):
    o_qk = 4 * NA_WIDTH
    o_vzo = o_qk + 2 * ML_WIDTH
    o_g = o_vzo + 3 * ML_WIDTH
    o_mg = o_g + N_GATES
    wb = w_in.astype(BF16)
    b = b_in.astype(F32)[None, :]
    wg = wb[:, o_g:o_mg]
    bg = b[:, o_g:o_mg]
    return (wb[:, :o_qk], wb[:, o_qk:o_vzo], wb[:, o_vzo:o_g], wb[:, o_mg:], wg, wg.T,
            b[:, :o_qk], b[:, o_qk:o_vzo], b[:, o_vzo:o_g], b[:, o_mg:], bg, bg.T)


def _trunk(x, meta_proj, wts, bias_tab, g_pre, conv_w, head_g, w_a, w_b, w_out, g_post):
    batch, seq, _ = x.shape
    assert seq % ML_BLOCK == 0 and seq % PROJ_ROWS == 0 and ML_BLOCK % ML_CHUNK == 0
    assert PROJ_ROWS % ML_CHUNK == 0
    x2d = x.reshape(batch * seq, D_MODEL)
    m_na, m_qk, m_vzo, _, m_gc, m_gr = meta_proj
    na, qk, vzo, mg, gc, gr = _proj(x2d, g_pre, wts, tm=PROJ_ROWS, chunk=ML_CHUNK)
    ya = _na(na, m_na, bias_tab, batch=batch, seq=seq)
    hf, qc, kc = _ml_fwd(qk, vzo, gc, gr, m_qk, m_vzo, m_gc, m_gr, conv_w, batch=batch, seq=seq, tb=ML_BLOCK)
    yb = _ml_bwd(qc, kc, vzo, gc, gr, hf, head_g, batch=batch, seq=seq, tb=ML_BLOCK)
    y = _out(x2d, ya, yb, mg, w_a, w_b, w_out, g_post, tm=PROJ_ROWS)
    return y.reshape(batch, seq, D_MODEL)


def kernel(x_prompt, x_sample, meta_tokens, g_pre, w_in, b_in, na_rpb, ml_conv_w, ml_head_g, w_a, w_b, w_out, g_post):
    assert w_in.shape[0] == 1, "the meta-token outputs are only droppable for a single layer"
    wts = _split_in_proj(w_in[0], b_in[0])
    gp = g_pre[0].astype(F32)[None, :]
    meta_proj = _proj(meta_tokens.astype(F32), gp, wts, tm=N_META, chunk=None)
    bias_tab = _na_bias_table(na_rpb[0])
    args = (meta_proj, wts, bias_tab, gp, ml_conv_w[0].astype(F32), ml_head_g[0].astype(F32)[None, :],
            w_a[0].astype(BF16), w_b[0].astype(BF16), w_out[0].astype(BF16), g_post[0].astype(F32)[None, :])
    return (_trunk(x_prompt, *args), _trunk(x_sample, *args))
```

```python
import functools

import jax
import jax.numpy as jnp
import numpy as np
from jax import lax
from jax.experimental import pallas as pl
from jax.experimental.pallas import tpu as pltpu

D_MODEL = 1024
N_META = 16
GRID_W = 64
EPS = 1e-6
NEG = -1e30
NA_HEADS = 8
NA_DH = 64
NA_WIDTH = NA_HEADS * NA_DH
NA_PAIR = 2 * NA_DH
NA_WIN_ROWS = 8
NA_WIN_COLS = 16
NA_ROWS_PER_STEP = 4
ML_HEADS = 4
ML_DH = 128
ML_WIDTH = ML_HEADS * ML_DH
N_GATES = 4 * ML_HEADS
ML_CHUNK = 128
ML_BLOCK = 256
PROJ_ROWS = 512
VMEM_LIMIT_BYTES = 56 * 1024 * 1024

F32 = jnp.float32
BF16 = jnp.bfloat16


def _dot(a, b):
    return jnp.dot(a, b, preferred_element_type=F32)


def _dot_nt(a, b):
    return lax.dot_general(a, b, (((1,), (1,)), ((), ())), preferred_element_type=F32)


def _dot_tn(a, b):
    return lax.dot_general(a, b, (((0,), (0,)), ((), ())), preferred_element_type=F32)


def _split3(x):
    hi = x.astype(BF16)
    r1 = x - hi.astype(F32)
    mid = r1.astype(BF16)
    lo = (r1 - mid.astype(F32)).astype(BF16)
    return hi, mid, lo


def _silu(x):
    return x * jax.nn.sigmoid(x)


def _log_sigmoid(x):
    return jnp.minimum(x, 0.0) - jnp.log1p(jnp.exp(-jnp.abs(x)))


def _resident(a):
    return pl.BlockSpec(a.shape, lambda *_: (0,) * a.ndim, pipeline_mode=pl.Buffered(1))


def _log2(n):
    assert n & (n - 1) == 0
    return n.bit_length() - 1


def _params(n_grid_axes):
    return pltpu.CompilerParams(dimension_semantics=("arbitrary",) * n_grid_axes,
                                vmem_limit_bytes=VMEM_LIMIT_BYTES)


def _proj_kernel(x_ref, g_ref, wna_ref, wqk_ref, wvzo_ref, wmg_ref, wg_ref, wgt_ref,
                 bna_ref, bqk_ref, bvzo_ref, bmg_ref, bg_ref, bgt_ref,
                 na_ref, qk_ref, vzo_ref, mg_ref, gc_ref, gr_ref, *, chunk):
    x = x_ref[...]
    xn = x * lax.rsqrt(jnp.mean(x * x, axis=-1, keepdims=True) + EPS) * g_ref[...]
    xb = xn.astype(BF16)
    na_ref[...] = (_dot(xb, wna_ref[...]) + bna_ref[...]).astype(BF16)
    qk_ref[...] = _dot(xb, wqk_ref[...]) + bqk_ref[...]
    vzo_ref[...] = (_dot(xb, wvzo_ref[...]) + bvzo_ref[...]).astype(BF16)
    mg_ref[...] = (_dot(xb, wmg_ref[...]) + bmg_ref[...]).astype(BF16)

    tm = x.shape[0]
    gc = _dot(xb, wg_ref[...]) + bg_ref[...]
    gr = _dot_nt(wgt_ref[...], xb) + bgt_ref[...]
    kind_c = lax.broadcasted_iota(jnp.int32, gc.shape, 1) >> _log2(ML_HEADS)
    kind_r = lax.broadcasted_iota(jnp.int32, gr.shape, 0) >> _log2(ML_HEADS)
    lfc = jnp.where((kind_c & 1) == 1, _log_sigmoid(gc), gc)
    lfr = jnp.where((kind_r & 1) == 1, _log_sigmoid(gr), gr)
    if chunk is None:
        gc_ref[...] = lfc
        gr_ref[...] = lfr
        return
    ri = lax.broadcasted_iota(jnp.int32, (chunk, chunk), 0)
    ci = lax.broadcasted_iota(jnp.int32, (chunk, chunk), 1)
    lower = jnp.where(ci <= ri, 1.0, 0.0).astype(BF16)
    upper = jnp.where(ci >= ri, 1.0, 0.0).astype(BF16)
    both = jnp.concatenate([lower, upper], axis=0)
    kc = lax.broadcasted_iota(jnp.int32, (chunk, N_GATES), 1) >> _log2(ML_HEADS)
    kr = lax.broadcasted_iota(jnp.int32, (N_GATES, chunk), 0) >> _log2(ML_HEADS)
    for c in range(tm // chunk):
        rows = slice(c * chunk, (c + 1) * chunk)
        pieces = _split3(lfr[:, rows])
        cs = sum(_dot_nt(both, piece) for piece in pieces)
        gc_ref[rows, :] = jnp.where(kc == 1, cs[:chunk], jnp.where(kc == 3, cs[chunk:], gc[rows]))
        pre_r = sum(_dot(piece, upper) for piece in pieces)
        suf_r = sum(_dot(piece, lower) for piece in pieces)
        gr_ref[:, rows] = jnp.where(kr == 1, pre_r, jnp.where(kr == 3, suf_r, gr[:, rows]))


def _proj(x2d, g_pre, wts, *, tm, chunk):
    rows = x2d.shape[0]
    assert rows % tm == 0
    row = lambda i: (i, 0)
    in_specs = [pl.BlockSpec((tm, D_MODEL), row), _resident(g_pre)] + [_resident(a) for a in wts]
    out_shape = (
        jax.ShapeDtypeStruct((rows, 4 * NA_WIDTH), BF16),
        jax.ShapeDtypeStruct((rows, 2 * ML_WIDTH), F32),
        jax.ShapeDtypeStruct((rows, 3 * ML_WIDTH), BF16),
        jax.ShapeDtypeStruct((rows, 2 * D_MODEL), BF16),
        jax.ShapeDtypeStruct((rows, N_GATES), F32),
        jax.ShapeDtypeStruct((N_GATES, rows), F32),
    )
    out_specs = (
        pl.BlockSpec((tm, 4 * NA_WIDTH), row),
        pl.BlockSpec((tm, 2 * ML_WIDTH), row),
        pl.BlockSpec((tm, 3 * ML_WIDTH), row),
        pl.BlockSpec((tm, 2 * D_MODEL), row),
        pl.BlockSpec((tm, N_GATES), row),
        pl.BlockSpec((N_GATES, tm), lambda i: (0, i)),
    )
    return pl.pallas_call(
        functools.partial(_proj_kernel, chunk=chunk),
        grid=(rows // tm,),
        in_specs=in_specs,
        out_specs=out_specs,
        out_shape=out_shape,
        compiler_params=_params(1),
        name="proj",
    )(x2d, g_pre, *wts)


def _na_row_start(r, n_rows):
    return jnp.clip(r - NA_WIN_ROWS // 2, 0, n_rows - NA_WIN_ROWS)


def _na_kernel(q_ref, k_ref, v_ref, z_ref, mk_ref, mv_ref, bias_ref, o_ref, *, n_rows, rows_per_step):
    n_keys = NA_WIN_ROWS * GRID_W
    scale = NA_DH ** -0.5
    lane = lax.broadcasted_iota(jnp.int32, (GRID_W, NA_PAIR), 1)
    lo = lane < NA_DH
    n_pairs = NA_WIDTH // NA_PAIR
    groups = [slice(g * NA_PAIR, (g + 1) * NA_PAIR) for g in range(n_pairs)]
    for i in range(rows_per_step):
        r = pl.program_id(1) * rows_per_step + i
        r0 = _na_row_start(r, n_rows)
        start = pl.multiple_of(r0 * GRID_W, GRID_W)
        off = r0 - r + NA_WIN_ROWS - 1
        qrows = slice(i * GRID_W, (i + 1) * GRID_W)
        scores = []
        for cols in groups:
            qf = q_ref[qrows, cols].astype(F32) * scale
            q2 = jnp.concatenate([jnp.where(lo, qf, 0.0), jnp.where(lo, 0.0, qf)], axis=0).astype(BF16)
            s = _dot_nt(q2, k_ref[pl.ds(start, n_keys), cols])
            sm = _dot_nt(q2, mk_ref[:, cols])
            scores.append((s, sm))
        probs = []
        for g, (s, sm) in enumerate(scores):
            bias = bias_ref[off, g]
            s = jnp.where(bias > 0.5 * NEG, s + bias, NEG)
            m = jnp.maximum(jnp.max(s, axis=-1, keepdims=True), jnp.max(sm, axis=-1, keepdims=True))
            p = jnp.exp(s - m)
            pm = jnp.exp(sm - m)
            denom = jnp.sum(p, axis=-1, keepdims=True) + jnp.sum(pm, axis=-1, keepdims=True)
            probs.append((p.astype(BF16), pm.astype(BF16), denom))
        for cols, (p, pm, denom) in zip(groups, probs):
            o2 = _dot(p, v_ref[pl.ds(start, n_keys), cols]) + _dot(pm, mv_ref[:, cols])
            o2 = o2 / denom
            o = jnp.where(lo, o2[0:GRID_W], o2[GRID_W:2 * GRID_W])
            o_ref[qrows, cols] = (o * _silu(z_ref[qrows, cols].astype(F32))).astype(BF16)


def _na_bias_table(rpb):
    c = np.arange(GRID_W)
    c0 = np.clip(c - NA_WIN_COLS // 2, 0, GRID_W - NA_WIN_COLS)
    kc = np.arange(GRID_W)
    valid = (kc[None, :] >= c0[:, None]) & (kc[None, :] < c0[:, None] + NA_WIN_COLS)
    dc = kc[None, :] - c[:, None] + NA_WIN_COLS - 1
    n_dc = 2 * NA_WIN_COLS - 1
    n_dr = 2 * NA_WIN_ROWS - 1
    onehot = ((dc[None] == np.arange(n_dc)[:, None, None]) & valid[None]).astype(np.float32)
    toe = jnp.dot(rpb.astype(F32).reshape(NA_HEADS * n_dr, n_dc), onehot.reshape(n_dc, GRID_W * GRID_W),
                  precision=lax.Precision.HIGHEST)
    toe = jnp.where(valid.reshape(1, GRID_W * GRID_W), toe, NEG)
    toe = toe.reshape(NA_HEADS, n_dr, GRID_W, GRID_W)
    tab = jnp.stack([toe[:, off:off + NA_WIN_ROWS] for off in range(NA_WIN_ROWS)])
    tab = jnp.transpose(tab, (0, 1, 3, 2, 4))
    return tab.reshape(NA_WIN_ROWS, NA_WIDTH // NA_PAIR, 2 * GRID_W, NA_WIN_ROWS * GRID_W)


def _na(na, meta_na, bias_tab, *, batch, seq):
    n_rows = seq // GRID_W
    rps = NA_ROWS_PER_STEP
    assert n_rows >= NA_WIN_ROWS and seq % GRID_W == 0 and n_rows % rps == 0
    steps = n_rows // rps
    qz = lambda col: pl.BlockSpec((rps * GRID_W, NA_WIDTH), lambda b, r: (b * steps + r, col))
    kv = lambda col: pl.BlockSpec((seq, NA_WIDTH), lambda b, r: (b, col), pipeline_mode=pl.Buffered(1))
    meta = lambda col: pl.BlockSpec((N_META, NA_WIDTH), lambda b, r: (0, col), pipeline_mode=pl.Buffered(1))
    return pl.pallas_call(
        functools.partial(_na_kernel, n_rows=n_rows, rows_per_step=rps),
        grid=(batch, steps),
        in_specs=[qz(0), kv(1), kv(2), qz(3), meta(1), meta(2), _resident(bias_tab)],
        out_specs=qz(0),
        out_shape=jax.ShapeDtypeStruct((batch * seq, NA_WIDTH), BF16),
        compiler_params=_params(2),
        name="na",
    )(na, na, na, na, meta_na, meta_na, bias_tab)


def _conv_silu(x_ref, prev8, next8, w, xs_ref):
    tb = x_ref.shape[0]
    xs_ref[0:8, :] = prev8
    xs_ref[8:tb + 8, :] = x_ref[...]
    xs_ref[tb + 8:tb + 16, :] = next8
    y = (w[0:1] * xs_ref[7:tb + 7, :] + w[1:2] * xs_ref[8:tb + 8, :]
         + w[2:3] * xs_ref[9:tb + 9, :])
    return _silu(y)


def _twice(row):
    return jnp.concatenate([row, row], axis=1)


def _ml_block(q_all, k_all, v_ref, gc_ref, gr_ref, ct_ref, m_ref, emit, *, reverse):
    tb = q_all.shape[0]
    ch = ML_CHUNK
    n_chunks = tb // ch
    order = range(n_chunks - 1, -1, -1) if reverse else range(n_chunks)
    ig_i, b_i = (2 * ML_HEADS, 3 * ML_HEADS) if reverse else (0, ML_HEADS)
    ones = jnp.ones((ch, ML_DH), BF16)
    ri = lax.broadcasted_iota(jnp.int32, (ch, ch), 0)
    ci = lax.broadcasted_iota(jnp.int32, (ch, ch), 1)
    causal = (ci >= ri) if reverse else (ci <= ri)
    eye = ci == ri
    heads = range(ML_HEADS)
    cts = [ct_ref[hd] for hd in heads]
    ms = [m_ref[hd][0:1, :] for hd in heads]
    work = []
    for c in order:
        rows = slice(c * ch, (c + 1) * ch)
        for hd in heads:
            cols = slice(hd * ML_DH, (hd + 1) * ML_DH)
            bc = jnp.broadcast_to(gc_ref[rows, b_i + hd:b_i + hd + 1], (ch, ch))
            r = gr_ref[ig_i + hd:ig_i + hd + 1, rows] - gr_ref[b_i + hd:b_i + hd + 1, rows]
            g = bc[0:1] if reverse else bc[ch - 1:ch]
            logd = jnp.where(causal, bc + jnp.broadcast_to(r, (ch, ch)), NEG)
            rowmax = jnp.max(logd, axis=1, keepdims=True)
            a = g + r
            m = ms[hd]
            m_new = jnp.maximum(g + m, jnp.max(a, axis=1, keepdims=True))
            ms[hd] = m_new
            w_diag = jnp.where(eye, jnp.exp(a - m_new), 0.0).astype(BF16)
            work.append(dict(hd=hd, rows=rows, cols=cols, bc=bc, logd=logd, rowmax=rowmax, m=m,
                             fs=jnp.exp(g + m - m_new), w_diag=w_diag,
                             k=k_all[rows, cols].astype(BF16), q=q_all[rows, cols].astype(BF16),
                             v_ext=jnp.concatenate([v_ref[rows, cols], ones], axis=1)))
    for w in work:
        w["wv"] = _dot(w["w_diag"], w["v_ext"]).astype(BF16)
    for w in work:
        w["qk"] = _dot_nt(w["q"], w["k"])
    for w in work:
        w["dct"] = _dot_tn(w["k"], w["wv"])
    for w in work:
        w["r2"] = _dot(w["q"], cts[w["hd"]].astype(BF16))
        cts[w["hd"]] = _twice(w["fs"]) * cts[w["hd"]] + w["dct"]
    for w in work:
        inter = w["bc"] + w["m"]
        m_t = jnp.maximum(w["rowmax"], inter)
        w["e_inter"] = jnp.exp(inter - m_t)
        w["floor"] = jnp.exp(-m_t)
        s = (w["qk"] * jnp.exp(w["logd"] - m_t)).astype(BF16)
        w["r1"] = _dot(s, w["v_ext"])
    for w in work:
        num = w["r1"][:, :ML_DH] + w["e_inter"] * w["r2"][:, :ML_DH]
        den = w["r1"][:, ML_DH:] + w["e_inter"] * w["r2"][:, ML_DH:]
        emit(w["hd"], w["rows"], w["cols"], num / jnp.maximum(jnp.abs(den), w["floor"]))
    for hd in heads:
        ct_ref[hd] = cts[hd]
        m_ref[hd] = jnp.broadcast_to(ms[hd], m_ref.shape[1:])


def _ml_fwd_kernel(qp_ref, kp_ref, qprev_ref, kprev_ref, qnext_ref, knext_ref, v_ref, gc_ref, gr_ref,
                   mqk_ref, mv_ref, mgc_ref, mgr_ref, cw_ref, hf_ref, qc_ref, kc_ref,
                   ct_ref, m_ref, xs_ref, *, n_blocks):
    i = pl.program_id(1)

    @pl.when(i == 0)
    def _():
        w = cw_ref[:, ML_WIDTH:2 * ML_WIDTH]
        xs_ref[0:8, :] = jnp.zeros((8, ML_WIDTH), F32)
        xs_ref[8:8 + N_META, :] = mqk_ref[:, ML_WIDTH:2 * ML_WIDTH]
        xs_ref[8 + N_META:16 + N_META, :] = kp_ref[0:8, :]
        y = (w[0:1] * xs_ref[7:7 + N_META, :] + w[1:2] * xs_ref[8:8 + N_META, :]
             + w[2:3] * xs_ref[9:9 + N_META, :])
        mk = _silu(y) * (ML_DH ** -0.5)
        ones = jnp.ones((N_META, ML_DH), BF16)
        ri = lax.broadcasted_iota(jnp.int32, (N_META, N_META), 0)
        ci = lax.broadcasted_iota(jnp.int32, (N_META, N_META), 1)
        for hd in range(ML_HEADS):
            cols = slice(hd * ML_DH, (hd + 1) * ML_DH)
            ig = mgc_ref[:, hd:hd + 1]
            lf = mgr_ref[ML_HEADS + hd:ML_HEADS + hd + 1, :]
            a = jnp.sum(jnp.where(ci > ri, lf, 0.0), axis=1, keepdims=True) + ig
            g = jnp.sum(lf, axis=1, keepdims=True)
            m_loc = jnp.max(a, axis=0, keepdims=True)
            m0 = jnp.maximum(g, m_loc)
            v_ext = jnp.concatenate([mv_ref[:, cols], ones], axis=1)
            dct = _dot_tn((jnp.exp(a - m_loc) * mk[:, cols]).astype(BF16), v_ext)
            ct_ref[hd] = jnp.exp(m_loc - m0) * dct
            m_ref[hd] = jnp.broadcast_to(m0, m_ref.shape[1:])

    first = i == 0
    last = i == n_blocks - 1
    zeros8 = jnp.zeros((8, ML_WIDTH), F32)
    qprev = jnp.where(first, mqk_ref[8:16, 0:ML_WIDTH], qprev_ref[...])
    kprev = jnp.where(first, mqk_ref[8:16, ML_WIDTH:2 * ML_WIDTH], kprev_ref[...])
    qnext = jnp.where(last, zeros8, qnext_ref[...])
    knext = jnp.where(last, zeros8, knext_ref[...])

    def emit(hd, rows, cols, h):
        hf_ref[rows, cols] = h

    q_all = _conv_silu(qp_ref, qprev, qnext, cw_ref[:, 0:ML_WIDTH], xs_ref).astype(BF16)
    k_all = _conv_silu(kp_ref, kprev, knext, cw_ref[:, ML_WIDTH:2 * ML_WIDTH], xs_ref) * (ML_DH ** -0.5)
    qc_ref[...] = q_all
    kc_ref[...] = k_all.astype(BF16)
    _ml_block(q_all, k_all, v_ref, gc_ref, gr_ref, ct_ref, m_ref, emit, reverse=False)


def _ml_bwd_kernel(qc_ref, kc_ref, v_ref, gc_ref, gr_ref, hf_ref, o_ref, z_ref, hg_ref, y_ref,
                   ct_ref, m_ref):
    @pl.when(pl.program_id(1) == 0)
    def _():
        ct_ref[...] = jnp.zeros(ct_ref.shape, F32)
        m_ref[...] = jnp.zeros(m_ref.shape, F32)

    def emit(hd, rows, cols, h):
        hh = jax.nn.sigmoid(o_ref[rows, cols].astype(F32)) * (hf_ref[rows, cols] + h)
        mu = jnp.mean(hh, axis=-1, keepdims=True)
        d = hh - mu
        var = jnp.mean(d * d, axis=-1, keepdims=True)
        y = d * lax.rsqrt(var + EPS) * hg_ref[:, cols] * _silu(z_ref[rows, cols].astype(F32))
        y_ref[rows, cols] = y.astype(BF16)

    _ml_block(qc_ref[...], kc_ref[...], v_ref, gc_ref, gr_ref, ct_ref, m_ref, emit, reverse=True)


def _ml_specs(batch, seq, tb, blk):
    nb = seq // tb
    n8 = tb // 8
    total8 = batch * seq // 8

    def main(col):
        return pl.BlockSpec((tb, ML_WIDTH), lambda b, j: (b * nb + blk(b, j), col))

    def prev(col):
        return pl.BlockSpec((8, ML_WIDTH), lambda b, j: (jnp.maximum((b * nb + blk(b, j)) * n8 - 1, 0), col))

    def nxt(col):
        return pl.BlockSpec((8, ML_WIDTH),
                            lambda b, j: (jnp.minimum((b * nb + blk(b, j) + 1) * n8, total8 - 1), col))

    gc = pl.BlockSpec((tb, N_GATES), lambda b, j: (b * nb + blk(b, j), 0))
    gr = pl.BlockSpec((N_GATES, tb), lambda b, j: (0, b * nb + blk(b, j)))
    return main, prev, nxt, gc, gr


def _ml_state_scratch():
    return [pltpu.VMEM((ML_HEADS, ML_DH, 2 * ML_DH), F32), pltpu.VMEM((ML_HEADS, 8, 128), F32)]


def _ml_fwd(qk, vzo, gc, gr, meta_qk, meta_vzo, meta_gc, meta_gr, conv_w, *, batch, seq, tb):
    nb = seq // tb
    main, prev, nxt, gc_spec, gr_spec = _ml_specs(batch, seq, tb, lambda b, j: j)
    const = lambda b, j: (0, 0)
    rows = batch * seq
    return pl.pallas_call(
        functools.partial(_ml_fwd_kernel, n_blocks=nb),
        grid=(batch, nb),
        in_specs=[main(0), main(1), prev(0), prev(1), nxt(0), nxt(1), main(0), gc_spec, gr_spec,
                  pl.BlockSpec(meta_qk.shape, const),
                  pl.BlockSpec((N_META, ML_WIDTH), const),
                  pl.BlockSpec(meta_gc.shape, const),
                  pl.BlockSpec(meta_gr.shape, const),
                  pl.BlockSpec(conv_w.shape, const)],
        out_specs=(main(0), main(0), main(0)),
        out_shape=(jax.ShapeDtypeStruct((rows, ML_WIDTH), F32),
                   jax.ShapeDtypeStruct((rows, ML_WIDTH), BF16),
                   jax.ShapeDtypeStruct((rows, ML_WIDTH), BF16)),
        scratch_shapes=_ml_state_scratch() + [pltpu.VMEM((tb + 16, ML_WIDTH), F32)],
        compiler_params=_params(2),
        name="ml_fwd",
    )(qk, qk, qk, qk, qk, qk, vzo, gc, gr, meta_qk, meta_vzo, meta_gc, meta_gr, conv_w)


def _ml_bwd(qc, kc, vzo, gc, gr, hf, head_g, *, batch, seq, tb):
    nb = seq // tb
    main, _, _, gc_spec, gr_spec = _ml_specs(batch, seq, tb, lambda b, j: nb - 1 - j)
    const = lambda b, j: (0, 0)
    return pl.pallas_call(
        _ml_bwd_kernel,
        grid=(batch, nb),
        in_specs=[main(0), main(0), main(0), gc_spec, gr_spec, main(0), main(2), main(1),
                  pl.BlockSpec(head_g.shape, const)],
        out_specs=main(0),
        out_shape=jax.ShapeDtypeStruct((batch * seq, ML_WIDTH), BF16),
        scratch_shapes=_ml_state_scratch(),
        compiler_params=_params(2),
        name="ml_bwd",
    )(qc, kc, vzo, gc, gr, hf, vzo, vzo, head_g)


def _out_kernel(x_ref, ya_ref, yb_ref, mg_ref, wa_ref, wb_ref, wo_ref, g_ref, o_ref):
    ya = _dot(ya_ref[...], wa_ref[...])
    yb = _dot(yb_ref[...], wb_ref[...])
    ga = jax.nn.sigmoid(mg_ref[:, 0:D_MODEL].astype(F32))
    gb = jax.nn.sigmoid(mg_ref[:, D_MODEL:2 * D_MODEL].astype(F32))
    y = _dot((ga * ya + gb * yb).astype(BF16), wo_ref[...])
    yn = y * lax.rsqrt(jnp.mean(y * y, axis=-1, keepdims=True) + EPS) * g_ref[...]
    o_ref[...] = x_ref[...] + yn


def _out(x2d, ya, yb, mg, w_a, w_b, w_out, g_post, *, tm):
    rows = x2d.shape[0]
    row = lambda i: (i, 0)
    return pl.pallas_call(
        _out_kernel,
        grid=(rows // tm,),
        in_specs=[pl.BlockSpec((tm, D_MODEL), row),
                  pl.BlockSpec((tm, NA_WIDTH), row),
                  pl.BlockSpec((tm, ML_WIDTH), row),
                  pl.BlockSpec((tm, 2 * D_MODEL), row),
                  _resident(w_a), _resident(w_b), _resident(w_out), _resident(g_post)],
        out_specs=pl.BlockSpec((tm, D_MODEL), row),
        out_shape=jax.ShapeDtypeStruct((rows, D_MODEL), F32),
        compiler_params=_params(1),
        name="out",
    )(x2d, ya, yb, mg, w_a, w_b, w_out, g_post)


def _split_in_proj(w_in, b_in):
    o_qk = 4 * NA_WIDTH
    o_vzo = o_qk + 2 * ML_WIDTH
    o_g = o_vzo + 3 * ML_WIDTH
    o_mg = o_g + N_GATES
    wb = w_in.astype(BF16)
    b = b_in.astype(F32)[None, :]
    wg = wb[:, o_g:o_mg]
    bg = b[:, o_g:o_mg]
    return (wb[:, :o_qk], wb[:, o_qk:o_vzo], wb[:, o_vzo:o_g], wb[:, o_mg:], wg, wg.T,
            b[:, :o_qk], b[:, o_qk:o_vzo], b[:, o_vzo:o_g], b[:, o_mg:], bg, bg.T)


def _trunk(x, meta_proj, wts, bias_tab, g_pre, conv_w, head_g, w_a, w_b, w_out, g_post):
    batch, seq, _ = x.shape
    assert seq % ML_BLOCK == 0 and seq % PROJ_ROWS == 0 and ML_BLOCK % ML_CHUNK == 0
    assert PROJ_ROWS % ML_CHUNK == 0
    x2d = x.reshape(batch * seq, D_MODEL)
    m_na, m_qk, m_vzo, _, m_gc, m_gr = meta_proj
    na, qk, vzo, mg, gc, gr = _proj(x2d, g_pre, wts, tm=PROJ_ROWS, chunk=ML_CHUNK)
    ya = _na(na, m_na, bias_tab, batch=batch, seq=seq)
    hf, qc, kc = _ml_fwd(qk, vzo, gc, gr, m_qk, m_vzo, m_gc, m_gr, conv_w, batch=batch, seq=seq, tb=ML_BLOCK)
    yb = _ml_bwd(qc, kc, vzo, gc, gr, hf, head_g, batch=batch, seq=seq, tb=ML_BLOCK)
    y = _out(x2d, ya, yb, mg, w_a, w_b, w_out, g_post, tm=PROJ_ROWS)
    return y.reshape(batch, seq, D_MODEL)


def kernel(x_prompt, x_sample, meta_tokens, g_pre, w_in, b_in, na_rpb, ml_conv_w, ml_head_g, w_a, w_b, w_out, g_post):
    assert w_in.shape[0] == 1, "the meta-token outputs are only droppable for a single layer"
    wts = _split_in_proj(w_in[0], b_in[0])
    gp = g_pre[0].astype(F32)[None, :]
    meta_proj = _proj(meta_tokens.astype(F32), gp, wts, tm=N_META, chunk=None)
    bias_tab = _na_bias_table(na_rpb[0])
    args = (meta_proj, wts, bias_tab, gp, ml_conv_w[0].astype(F32), ml_head_g[0].astype(F32)[None, :],
            w_a[0].astype(BF16), w_b[0].astype(BF16), w_out[0].astype(BF16), g_post[0].astype(F32)[None, :])
    return (_trunk(x_prompt, *args), _trunk(x_sample, *args))
```

```python
import functools

import jax
import jax.numpy as jnp
import numpy as np
from jax import lax
from jax.experimental import pallas as pl
from jax.experimental.pallas import tpu as pltpu

D_MODEL = 1024
N_META = 16
GRID_W = 64
EPS = 1e-6
NEG = -1e30
NA_HEADS = 8
NA_DH = 64
NA_WIDTH = NA_HEADS * NA_DH
NA_PAIR = 2 * NA_DH
NA_WIN_ROWS = 8
NA_WIN_COLS = 16
NA_ROWS_PER_STEP = 4
ML_HEADS = 4
ML_DH = 128
ML_WIDTH = ML_HEADS * ML_DH
N_GATES = 4 * ML_HEADS
ML_CHUNK = 128
ML_BLOCK = 256
PROJ_ROWS = 512
VMEM_LIMIT_BYTES = 56 * 1024 * 1024

F32 = jnp.float32
BF16 = jnp.bfloat16


def _dot(a, b):
    return jnp.dot(a, b, preferred_element_type=F32)


def _dot_nt(a, b):
    return lax.dot_general(a, b, (((1,), (1,)), ((), ())), preferred_element_type=F32)


def _dot_tn(a, b):
    return lax.dot_general(a, b, (((0,), (0,)), ((), ())), preferred_element_type=F32)


def _split3(x):
    hi = x.astype(BF16)
    r1 = x - hi.astype(F32)
    mid = r1.astype(BF16)
    lo = (r1 - mid.astype(F32)).astype(BF16)
    return hi, mid, lo


def _sigmoid(x):
    return 0.5 * jnp.tanh(0.5 * x) + 0.5


def _silu(x):
    return x * _sigmoid(x)


def _log_sigmoid(x):
    return jnp.minimum(x, 0.0) - jnp.log1p(jnp.exp(-jnp.abs(x)))


def _resident(a):
    return pl.BlockSpec(a.shape, lambda *_: (0,) * a.ndim, pipeline_mode=pl.Buffered(1))


def _log2(n):
    assert n & (n - 1) == 0
    return n.bit_length() - 1


def _params(n_grid_axes):
    return pltpu.CompilerParams(dimension_semantics=("arbitrary",) * n_grid_axes,
                                vmem_limit_bytes=VMEM_LIMIT_BYTES)


def _proj_kernel(x_ref, g_ref, wna_ref, wqk_ref, wvzo_ref, wmg_ref, wg_ref, wgt_ref,
                 bna_ref, bqk_ref, bvzo_ref, bmg_ref, bg_ref, bgt_ref,
                 na_ref, qk_ref, vzo_ref, mg_ref, gc_ref, gr_ref, *, chunk):
    x = x_ref[...]
    xn = x * lax.rsqrt(jnp.mean(x * x, axis=-1, keepdims=True) + EPS) * g_ref[...]
    xb = xn.astype(BF16)
    na_ref[...] = (_dot(xb, wna_ref[...]) + bna_ref[...]).astype(BF16)
    qk_ref[...] = _dot(xb, wqk_ref[...]) + bqk_ref[...]
    vzo_ref[...] = (_dot(xb, wvzo_ref[...]) + bvzo_ref[...]).astype(BF16)
    mg_ref[...] = (_dot(xb, wmg_ref[...]) + bmg_ref[...]).astype(BF16)

    tm = x.shape[0]
    gc = _dot(xb, wg_ref[...]) + bg_ref[...]
    gr = _dot_nt(wgt_ref[...], xb) + bgt_ref[...]
    kind_c = lax.broadcasted_iota(jnp.int32, gc.shape, 1) >> _log2(ML_HEADS)
    kind_r = lax.broadcasted_iota(jnp.int32, gr.shape, 0) >> _log2(ML_HEADS)
    lfc = jnp.where((kind_c & 1) == 1, _log_sigmoid(gc), gc)
    lfr = jnp.where((kind_r & 1) == 1, _log_sigmoid(gr), gr)
    if chunk is None:
        gc_ref[...] = lfc
        gr_ref[...] = lfr
        return
    ri = lax.broadcasted_iota(jnp.int32, (chunk, chunk), 0)
    ci = lax.broadcasted_iota(jnp.int32, (chunk, chunk), 1)
    lower = jnp.where(ci <= ri, 1.0, 0.0).astype(BF16)
    upper = jnp.where(ci >= ri, 1.0, 0.0).astype(BF16)
    both = jnp.concatenate([lower, upper], axis=0)
    kc = lax.broadcasted_iota(jnp.int32, (chunk, N_GATES), 1) >> _log2(ML_HEADS)
    kr = lax.broadcasted_iota(jnp.int32, (N_GATES, chunk), 0) >> _log2(ML_HEADS)
    for c in range(tm // chunk):
        rows = slice(c * chunk, (c + 1) * chunk)
        pieces = _split3(lfr[:, rows])
        cs = sum(_dot_nt(both, piece) for piece in pieces)
        gc_ref[rows, :] = jnp.where(kc == 1, cs[:chunk], jnp.where(kc == 3, cs[chunk:], gc[rows]))
        pre_r = sum(_dot(piece, upper) for piece in pieces)
        suf_r = sum(_dot(piece, lower) for piece in pieces)
        gr_ref[:, rows] = jnp.where(kr == 1, pre_r, jnp.where(kr == 3, suf_r, gr[:, rows]))


def _proj(x2d, g_pre, wts, *, tm, chunk):
    rows = x2d.shape[0]
    assert rows % tm == 0
    row = lambda i: (i, 0)
    in_specs = [pl.BlockSpec((tm, D_MODEL), row), _resident(g_pre)] + [_resident(a) for a in wts]
    out_shape = (
        jax.ShapeDtypeStruct((rows, 4 * NA_WIDTH), BF16),
        jax.ShapeDtypeStruct((rows, 2 * ML_WIDTH), F32),
        jax.ShapeDtypeStruct((rows, 3 * ML_WIDTH), BF16),
        jax.ShapeDtypeStruct((rows, 2 * D_MODEL), BF16),
        jax.ShapeDtypeStruct((rows, N_GATES), F32),
        jax.ShapeDtypeStruct((N_GATES, rows), F32),
    )
    out_specs = (
        pl.BlockSpec((tm, 4 * NA_WIDTH), row),
        pl.BlockSpec((tm, 2 * ML_WIDTH), row),
        pl.BlockSpec((tm, 3 * ML_WIDTH), row),
        pl.BlockSpec((tm, 2 * D_MODEL), row),
        pl.BlockSpec((tm, N_GATES), row),
        pl.BlockSpec((N_GATES, tm), lambda i: (0, i)),
    )
    return pl.pallas_call(
        functools.partial(_proj_kernel, chunk=chunk),
        grid=(rows // tm,),
        in_specs=in_specs,
        out_specs=out_specs,
        out_shape=out_shape,
        compiler_params=_params(1),
        name="proj",
    )(x2d, g_pre, *wts)


def _na_row_start(r, n_rows):
    return jnp.clip(r - NA_WIN_ROWS // 2, 0, n_rows - NA_WIN_ROWS)


def _na_kernel(q_ref, k_ref, v_ref, z_ref, mk_ref, mv_ref, bias_ref, o_ref, *, n_rows, rows_per_step):
    n_keys = NA_WIN_ROWS * GRID_W
    scale = NA_DH ** -0.5
    lane = lax.broadcasted_iota(jnp.int32, (GRID_W, NA_PAIR), 1)
    lo = lane < NA_DH
    n_pairs = NA_WIDTH // NA_PAIR
    groups = [slice(g * NA_PAIR, (g + 1) * NA_PAIR) for g in range(n_pairs)]

    def window(i):
        r = pl.program_id(1) * rows_per_step + i
        r0 = _na_row_start(r, n_rows)
        return pl.multiple_of(r0 * GRID_W, GRID_W), r0 - r + NA_WIN_ROWS - 1

    def scores(i):
        start, _ = window(i)
        qrows = slice(i * GRID_W, (i + 1) * GRID_W)
        out = []
        for cols in groups:
            qf = q_ref[qrows, cols].astype(F32) * scale
            q2 = jnp.concatenate([jnp.where(lo, qf, 0.0), jnp.where(lo, 0.0, qf)], axis=0).astype(BF16)
            s = _dot_nt(q2, k_ref[pl.ds(start, n_keys), cols])
            sm = _dot_nt(q2, mk_ref[:, cols])
            out.append((s, sm))
        return out

    def softmax(i, sc):
        _, off = window(i)
        out = []
        for g, (s, sm) in enumerate(sc):
            bias = bias_ref[off, g]
            s = jnp.where(bias > 0.5 * NEG, s + bias, NEG)
            m = jnp.maximum(jnp.max(s, axis=-1, keepdims=True), jnp.max(sm, axis=-1, keepdims=True))
            p = jnp.exp(s - m)
            pm = jnp.exp(sm - m)
            denom = jnp.sum(p, axis=-1, keepdims=True) + jnp.sum(pm, axis=-1, keepdims=True)
            out.append((p.astype(BF16), pm.astype(BF16), denom))
        return out

    def values(i, probs):
        start, _ = window(i)
        qrows = slice(i * GRID_W, (i + 1) * GRID_W)
        for cols, (p, pm, denom) in zip(groups, probs):
            o2 = _dot(p, v_ref[pl.ds(start, n_keys), cols]) + _dot(pm, mv_ref[:, cols])
            o2 = o2 / denom
            o = jnp.where(lo, o2[0:GRID_W], o2[GRID_W:2 * GRID_W])
            o_ref[qrows, cols] = (o * _silu(z_ref[qrows, cols].astype(F32))).astype(BF16)

    sc = scores(0)
    for i in range(rows_per_step):
        nxt = scores(i + 1) if i + 1 < rows_per_step else None
        values(i, softmax(i, sc))
        sc = nxt


def _na_bias_table(rpb):
    c = np.arange(GRID_W)
    c0 = np.clip(c - NA_WIN_COLS // 2, 0, GRID_W - NA_WIN_COLS)
    kc = np.arange(GRID_W)
    valid = (kc[None, :] >= c0[:, None]) & (kc[None, :] < c0[:, None] + NA_WIN_COLS)
    dc = kc[None, :] - c[:, None] + NA_WIN_COLS - 1
    n_dc = 2 * NA_WIN_COLS - 1
    n_dr = 2 * NA_WIN_ROWS - 1
    onehot = ((dc[None] == np.arange(n_dc)[:, None, None]) & valid[None]).astype(np.float32)
    toe = jnp.dot(rpb.astype(F32).reshape(NA_HEADS * n_dr, n_dc), onehot.reshape(n_dc, GRID_W * GRID_W),
                  precision=lax.Precision.HIGHEST)
    toe = jnp.where(valid.reshape(1, GRID_W * GRID_W), toe, NEG)
    toe = toe.reshape(NA_HEADS, n_dr, GRID_W, GRID_W)
    tab = jnp.stack([toe[:, off:off + NA_WIN_ROWS] for off in range(NA_WIN_ROWS)])
    tab = jnp.transpose(tab, (0, 1, 3, 2, 4))
    return tab.reshape(NA_WIN_ROWS, NA_WIDTH // NA_PAIR, 2 * GRID_W, NA_WIN_ROWS * GRID_W)


def _na(na, meta_na, bias_tab, *, batch, seq):
    n_rows = seq // GRID_W
    rps = NA_ROWS_PER_STEP
    assert n_rows >= NA_WIN_ROWS and seq % GRID_W == 0 and n_rows % rps == 0
    steps = n_rows // rps
    qz = lambda col: pl.BlockSpec((rps * GRID_W, NA_WIDTH), lambda b, r: (b * steps + r, col))
    kv = lambda col: pl.BlockSpec((seq, NA_WIDTH), lambda b, r: (b, col), pipeline_mode=pl.Buffered(1))
    meta = lambda col: pl.BlockSpec((N_META, NA_WIDTH), lambda b, r: (0, col), pipeline_mode=pl.Buffered(1))
    return pl.pallas_call(
        functools.partial(_na_kernel, n_rows=n_rows, rows_per_step=rps),
        grid=(batch, steps),
        in_specs=[qz(0), kv(1), kv(2), qz(3), meta(1), meta(2), _resident(bias_tab)],
        out_specs=qz(0),
        out_shape=jax.ShapeDtypeStruct((batch * seq, NA_WIDTH), BF16),
        compiler_params=_params(2),
        name="na",
    )(na, na, na, na, meta_na, meta_na, bias_tab)


def _conv_silu(x_ref, prev8, next8, w, xs_ref):
    tb = x_ref.shape[0]
    xs_ref[0:8, :] = prev8
    xs_ref[8:tb + 8, :] = x_ref[...]
    xs_ref[tb + 8:tb + 16, :] = next8
    y = (w[0:1] * xs_ref[7:tb + 7, :] + w[1:2] * xs_ref[8:tb + 8, :]
         + w[2:3] * xs_ref[9:tb + 9, :])
    return _silu(y)


def _twice(row):
    return jnp.concatenate([row, row], axis=1)


def _ml_block(q_all, k_all, v_ref, gc_ref, gr_ref, ct_ref, m_ref, emit, *, reverse):
    tb = q_all.shape[0]
    ch = ML_CHUNK
    n_chunks = tb // ch
    order = range(n_chunks - 1, -1, -1) if reverse else range(n_chunks)
    ig_i, b_i = (2 * ML_HEADS, 3 * ML_HEADS) if reverse else (0, ML_HEADS)
    ones = jnp.ones((ch, ML_DH), BF16)
    ri = lax.broadcasted_iota(jnp.int32, (ch, ch), 0)
    ci = lax.broadcasted_iota(jnp.int32, (ch, ch), 1)
    causal = (ci >= ri) if reverse else (ci <= ri)
    eye = ci == ri
    heads = range(ML_HEADS)
    cts = [ct_ref[hd] for hd in heads]
    ms = [m_ref[hd][0:1, :] for hd in heads]
    work = []
    for c in order:
        rows = slice(c * ch, (c + 1) * ch)
        for hd in heads:
            cols = slice(hd * ML_DH, (hd + 1) * ML_DH)
            bc = jnp.broadcast_to(gc_ref[rows, b_i + hd:b_i + hd + 1], (ch, ch))
            r = gr_ref[ig_i + hd:ig_i + hd + 1, rows] - gr_ref[b_i + hd:b_i + hd + 1, rows]
            g = bc[0:1] if reverse else bc[ch - 1:ch]
            logd = jnp.where(causal, bc + jnp.broadcast_to(r, (ch, ch)), NEG)
            rowmax = jnp.max(logd, axis=1, keepdims=True)
            a = g + r
            m = ms[hd]
            m_new = jnp.maximum(g + m, jnp.max(a, axis=1, keepdims=True))
            ms[hd] = m_new
            w_diag = jnp.where(eye, jnp.exp(a - m_new), 0.0).astype(BF16)
            work.append(dict(hd=hd, rows=rows, cols=cols, bc=bc, logd=logd, rowmax=rowmax, m=m,
                             fs=jnp.exp(g + m - m_new), w_diag=w_diag,
                             k=k_all[rows, cols].astype(BF16), q=q_all[rows, cols].astype(BF16),
                             v_ext=jnp.concatenate([v_ref[rows, cols], ones], axis=1)))
    for w in work:
        w["wv"] = _dot(w["w_diag"], w["v_ext"]).astype(BF16)
    for w in work:
        w["qk"] = _dot_nt(w["q"], w["k"])
    for w in work:
        w["dct"] = _dot_tn(w["k"], w["wv"])
    for w in work:
        w["r2"] = _dot(w["q"], cts[w["hd"]].astype(BF16))
        cts[w["hd"]] = _twice(w["fs"]) * cts[w["hd"]] + w["dct"]
    for w in work:
        inter = w["bc"] + w["m"]
        m_t = jnp.maximum(w["rowmax"], inter)
        w["e_inter"] = jnp.exp(inter - m_t)
        w["floor"] = jnp.exp(-m_t)
        s = (w["qk"] * jnp.exp(w["logd"] - m_t)).astype(BF16)
        w["r1"] = _dot(s, w["v_ext"])
    for w in work:
        num = w["r1"][:, :ML_DH] + w["e_inter"] * w["r2"][:, :ML_DH]
        den = w["r1"][:, ML_DH:] + w["e_inter"] * w["r2"][:, ML_DH:]
        emit(w["hd"], w["rows"], w["cols"], num / jnp.maximum(jnp.abs(den), w["floor"]))
    for hd in heads:
        ct_ref[hd] = cts[hd]
        m_ref[hd] = jnp.broadcast_to(ms[hd], m_ref.shape[1:])


def _ml_fwd_kernel(qp_ref, kp_ref, qprev_ref, kprev_ref, qnext_ref, knext_ref, v_ref, gc_ref, gr_ref,
                   mqk_ref, mv_ref, mgc_ref, mgr_ref, cw_ref, hf_ref, qc_ref, kc_ref,
                   ct_ref, m_ref, xs_ref, *, n_blocks):
    i = pl.program_id(1)

    @pl.when(i == 0)
    def _():
        w = cw_ref[:, ML_WIDTH:2 * ML_WIDTH]
        xs_ref[0:8, :] = jnp.zeros((8, ML_WIDTH), F32)
        xs_ref[8:8 + N_META, :] = mqk_ref[:, ML_WIDTH:2 * ML_WIDTH]
        xs_ref[8 + N_META:16 + N_META, :] = kp_ref[0:8, :]
        y = (w[0:1] * xs_ref[7:7 + N_META, :] + w[1:2] * xs_ref[8:8 + N_META, :]
             + w[2:3] * xs_ref[9:9 + N_META, :])
        mk = _silu(y) * (ML_DH ** -0.5)
        ones = jnp.ones((N_META, ML_DH), BF16)
        ri = lax.broadcasted_iota(jnp.int32, (N_META, N_META), 0)
        ci = lax.broadcasted_iota(jnp.int32, (N_META, N_META), 1)
        for hd in range(ML_HEADS):
            cols = slice(hd * ML_DH, (hd + 1) * ML_DH)
            ig = mgc_ref[:, hd:hd + 1]
            lf = mgr_ref[ML_HEADS + hd:ML_HEADS + hd + 1, :]
            a = jnp.sum(jnp.where(ci > ri, lf, 0.0), axis=1, keepdims=True) + ig
            g = jnp.sum(lf, axis=1, keepdims=True)
            m_loc = jnp.max(a, axis=0, keepdims=True)
            m0 = jnp.maximum(g, m_loc)
            v_ext = jnp.concatenate([mv_ref[:, cols], ones], axis=1)
            dct = _dot_tn((jnp.exp(a - m_loc) * mk[:, cols]).astype(BF16), v_ext)
            ct_ref[hd] = jnp.exp(m_loc - m0) * dct
            m_ref[hd] = jnp.broadcast_to(m0, m_ref.shape[1:])

    first = i == 0
    last = i == n_blocks - 1
    zeros8 = jnp.zeros((8, ML_WIDTH), F32)
    qprev = jnp.where(first, mqk_ref[8:16, 0:ML_WIDTH], qprev_ref[...])
    kprev = jnp.where(first, mqk_ref[8:16, ML_WIDTH:2 * ML_WIDTH], kprev_ref[...])
    qnext = jnp.where(last, zeros8, qnext_ref[...])
    knext = jnp.where(last, zeros8, knext_ref[...])

    def emit(hd, rows, cols, h):
        hf_ref[rows, cols] = h

    q_all = _conv_silu(qp_ref, qprev, qnext, cw_ref[:, 0:ML_WIDTH], xs_ref).astype(BF16)
    k_all = _conv_silu(kp_ref, kprev, knext, cw_ref[:, ML_WIDTH:2 * ML_WIDTH], xs_ref) * (ML_DH ** -0.5)
    qc_ref[...] = q_all
    kc_ref[...] = k_all.astype(BF16)
    _ml_block(q_all, k_all, v_ref, gc_ref, gr_ref, ct_ref, m_ref, emit, reverse=False)


def _merge_project(x, ya_in, yb_in, mg_ref, wa_ref, wb_ref, wo_ref, g_ref):
    ya = _dot(ya_in, wa_ref[...])
    yb = _dot(yb_in, wb_ref[...])
    ga = _sigmoid(mg_ref[:, 0:D_MODEL].astype(F32))
    gb = _sigmoid(mg_ref[:, D_MODEL:2 * D_MODEL].astype(F32))
    y = _dot((ga * ya + gb * yb).astype(BF16), wo_ref[...])
    return x + y * lax.rsqrt(jnp.mean(y * y, axis=-1, keepdims=True) + EPS) * g_ref[...]


def _ml_bwd_out_kernel(qc_ref, kc_ref, v_ref, gc_ref, gr_ref, hf_ref, o_ref, z_ref, hg_ref,
                       x_ref, ya_ref, mg_ref, wa_ref, wb_ref, wo_ref, g_ref, y_ref,
                       ct_ref, m_ref, yb_ref):
    @pl.when(pl.program_id(1) == 0)
    def _():
        ct_ref[...] = jnp.zeros(ct_ref.shape, F32)
        m_ref[...] = jnp.zeros(m_ref.shape, F32)
        yb_ref[...] = jnp.zeros(yb_ref.shape, BF16)

    y_ref[...] = _merge_project(x_ref[...], ya_ref[...], yb_ref[...], mg_ref, wa_ref, wb_ref, wo_ref, g_ref)

    def emit(hd, rows, cols, h):
        hh = _sigmoid(o_ref[rows, cols].astype(F32)) * (hf_ref[rows, cols] + h)
        mu = jnp.mean(hh, axis=-1, keepdims=True)
        d = hh - mu
        var = jnp.mean(d * d, axis=-1, keepdims=True)
        y = d * lax.rsqrt(var + EPS) * hg_ref[:, cols] * _silu(z_ref[rows, cols].astype(F32))
        yb_ref[rows, cols] = y.astype(BF16)

    _ml_block(qc_ref[...], kc_ref[...], v_ref, gc_ref, gr_ref, ct_ref, m_ref, emit, reverse=True)


def _ml_specs(batch, seq, tb, blk):
    nb = seq // tb
    n8 = tb // 8
    total8 = batch * seq // 8

    def main(col):
        return pl.BlockSpec((tb, ML_WIDTH), lambda b, j: (b * nb + blk(b, j), col))

    def prev(col):
        return pl.BlockSpec((8, ML_WIDTH), lambda b, j: (jnp.maximum((b * nb + blk(b, j)) * n8 - 1, 0), col))

    def nxt(col):
        return pl.BlockSpec((8, ML_WIDTH),
                            lambda b, j: (jnp.minimum((b * nb + blk(b, j) + 1) * n8, total8 - 1), col))

    gc = pl.BlockSpec((tb, N_GATES), lambda b, j: (b * nb + blk(b, j), 0))
    gr = pl.BlockSpec((N_GATES, tb), lambda b, j: (0, b * nb + blk(b, j)))
    return main, prev, nxt, gc, gr


def _ml_state_scratch():
    return [pltpu.VMEM((ML_HEADS, ML_DH, 2 * ML_DH), F32), pltpu.VMEM((ML_HEADS, 8, 128), F32)]


def _ml_fwd(qk, vzo, gc, gr, meta_qk, meta_vzo, meta_gc, meta_gr, conv_w, *, batch, seq, tb):
    nb = seq // tb
    main, prev, nxt, gc_spec, gr_spec = _ml_specs(batch, seq, tb, lambda b, j: j)
    const = lambda b, j: (0, 0)
    rows = batch * seq
    return pl.pallas_call(
        functools.partial(_ml_fwd_kernel, n_blocks=nb),
        grid=(batch, nb),
        in_specs=[main(0), main(1), prev(0), prev(1), nxt(0), nxt(1), main(0), gc_spec, gr_spec,
                  pl.BlockSpec(meta_qk.shape, const),
                  pl.BlockSpec((N_META, ML_WIDTH), const),
                  pl.BlockSpec(meta_gc.shape, const),
                  pl.BlockSpec(meta_gr.shape, const),
                  pl.BlockSpec(conv_w.shape, const)],
        out_specs=(main(0), main(0), main(0)),
        out_shape=(jax.ShapeDtypeStruct((rows, ML_WIDTH), F32),
                   jax.ShapeDtypeStruct((rows, ML_WIDTH), BF16),
                   jax.ShapeDtypeStruct((rows, ML_WIDTH), BF16)),
        scratch_shapes=_ml_state_scratch() + [pltpu.VMEM((tb + 16, ML_WIDTH), F32)],
        compiler_params=_params(2),
        name="ml_fwd",
    )(qk, qk, qk, qk, qk, qk, vzo, gc, gr, meta_qk, meta_vzo, meta_gc, meta_gr, conv_w)


def _ml_bwd_out(qc, kc, vzo, gc, gr, hf, head_g, x2d, ya, mg, w_a, w_b, w_out, g_post, *, batch, seq, tb):
    nb = seq // tb
    main, _, _, gc_spec, gr_spec = _ml_specs(batch, seq, tb, lambda b, j: nb - 1 - jnp.minimum(j, nb - 1))
    wide = lambda width: pl.BlockSpec((tb, width), lambda b, j: (b * nb + nb - jnp.maximum(j, 1), 0))
    return pl.pallas_call(
        _ml_bwd_out_kernel,
        grid=(batch, nb + 1),
        in_specs=[main(0), main(0), main(0), gc_spec, gr_spec, main(0), main(2), main(1), _resident(head_g),
                  wide(D_MODEL), wide(NA_WIDTH), wide(2 * D_MODEL),
                  _resident(w_a), _resident(w_b), _resident(w_out), _resident(g_post)],
        out_specs=wide(D_MODEL),
        out_shape=jax.ShapeDtypeStruct((batch * seq, D_MODEL), F32),
        scratch_shapes=_ml_state_scratch() + [pltpu.VMEM((tb, ML_WIDTH), BF16)],
        compiler_params=_params(2),
        name="ml_bwd_out",
    )(qc, kc, vzo, gc, gr, hf, vzo, vzo, head_g, x2d, ya, mg, w_a, w_b, w_out, g_post)


def _split_in_proj(w_in, b_in):
    o_qk = 4 * NA_WIDTH
    o_vzo = o_qk + 2 * ML_WIDTH
    o_g = o_vzo + 3 * ML_WIDTH
    o_mg = o_g + N_GATES
    wb = w_in.astype(BF16)
    b = b_in.astype(F32)[None, :]
    wg = wb[:, o_g:o_mg]
    bg = b[:, o_g:o_mg]
    return (wb[:, :o_qk], wb[:, o_qk:o_vzo], wb[:, o_vzo:o_g], wb[:, o_mg:], wg, wg.T,
            b[:, :o_qk], b[:, o_qk:o_vzo], b[:, o_vzo:o_g], b[:, o_mg:], bg, bg.T)


def _trunk(x, meta_proj, wts, bias_tab, g_pre, conv_w, head_g, w_a, w_b, w_out, g_post):
    batch, seq, _ = x.shape
    assert seq % ML_BLOCK == 0 and seq % PROJ_ROWS == 0 and ML_BLOCK % ML_CHUNK == 0
    assert PROJ_ROWS % ML_CHUNK == 0
    x2d = x.reshape(batch * seq, D_MODEL)
    m_na, m_qk, m_vzo, _, m_gc, m_gr = meta_proj
    na, qk, vzo, mg, gc, gr = _proj(x2d, g_pre, wts, tm=PROJ_ROWS, chunk=ML_CHUNK)
    ya = _na(na, m_na, bias_tab, batch=batch, seq=seq)
    hf, qc, kc = _ml_fwd(qk, vzo, gc, gr, m_qk, m_vzo, m_gc, m_gr, conv_w, batch=batch, seq=seq, tb=ML_BLOCK)
    y = _ml_bwd_out(qc, kc, vzo, gc, gr, hf, head_g, x2d, ya, mg, w_a, w_b, w_out, g_post,
                    batch=batch, seq=seq, tb=ML_BLOCK)
    return y.reshape(batch, seq, D_MODEL)


def kernel(x_prompt, x_sample, meta_tokens, g_pre, w_in, b_in, na_rpb, ml_conv_w, ml_head_g, w_a, w_b, w_out, g_post):
    assert w_in.shape[0] == 1, "the meta-token outputs are only droppable for a single layer"
    wts = _split_in_proj(w_in[0], b_in[0])
    gp = g_pre[0].astype(F32)[None, :]
    meta_proj = _proj(meta_tokens.astype(F32), gp, wts, tm=N_META, chunk=None)
    bias_tab = _na_bias_table(na_rpb[0])
    args = (meta_proj, wts, bias_tab, gp, ml_conv_w[0].astype(F32), ml_head_g[0].astype(F32)[None, :],
            w_a[0].astype(BF16), w_b[0].astype(BF16), w_out[0].astype(BF16), g_post[0].astype(F32)[None, :])
    return (_trunk(x_prompt, *args), _trunk(x_sample, *args))
```

```python
import functools

import jax
import jax.numpy as jnp
import numpy as np
from jax import lax
from jax.experimental import pallas as pl
from jax.experimental.pallas import tpu as pltpu

D_MODEL = 1024
N_META = 16
GRID_W = 64
EPS = 1e-6
NEG = -1e30
NA_HEADS = 8
NA_DH = 64
NA_WIDTH = NA_HEADS * NA_DH
NA_PAIR = 2 * NA_DH
NA_WIN_ROWS = 8
NA_WIN_COLS = 16
NA_ROWS_PER_STEP = 4
ML_HEADS = 4
ML_DH = 128
ML_WIDTH = ML_HEADS * ML_DH
N_GATES = 4 * ML_HEADS
ML_CHUNK = 128
ML_BLOCK = 256
VMEM_LIMIT_BYTES = 56 * 1024 * 1024

F32 = jnp.float32
BF16 = jnp.bfloat16


def _dot(a, b):
    return jnp.dot(a, b, preferred_element_type=F32)


def _dot_nt(a, b):
    return lax.dot_general(a, b, (((1,), (1,)), ((), ())), preferred_element_type=F32)


def _dot_tn(a, b):
    return lax.dot_general(a, b, (((0,), (0,)), ((), ())), preferred_element_type=F32)


def _split3(x):
    hi = x.astype(BF16)
    r1 = x - hi.astype(F32)
    mid = r1.astype(BF16)
    lo = (r1 - mid.astype(F32)).astype(BF16)
    return hi, mid, lo


def _sigmoid(x):
    return 0.5 * jnp.tanh(0.5 * x) + 0.5


def _silu(x):
    return x * _sigmoid(x)


def _log_sigmoid(x):
    return jnp.minimum(x, 0.0) - jnp.log1p(jnp.exp(-jnp.abs(x)))


def _resident(a):
    return pl.BlockSpec(a.shape, lambda *_: (0,) * a.ndim, pipeline_mode=pl.Buffered(1))


def _log2(n):
    assert n & (n - 1) == 0
    return n.bit_length() - 1


def _params(n_grid_axes):
    return pltpu.CompilerParams(dimension_semantics=("arbitrary",) * n_grid_axes,
                                vmem_limit_bytes=VMEM_LIMIT_BYTES)


def _rmsnorm_bf16(x, g):
    return (x * lax.rsqrt(jnp.mean(x * x, axis=-1, keepdims=True) + EPS) * g).astype(BF16)


def _gates(xb, wg_ref, wgt_ref, bg_ref, bgt_ref):
    gc = _dot(xb, wg_ref[...]) + bg_ref[...]
    gr = _dot_nt(wgt_ref[...], xb) + bgt_ref[...]
    kind_c = lax.broadcasted_iota(jnp.int32, gc.shape, 1) >> _log2(ML_HEADS)
    kind_r = lax.broadcasted_iota(jnp.int32, gr.shape, 0) >> _log2(ML_HEADS)
    return (jnp.where((kind_c & 1) == 1, _log_sigmoid(gc), gc),
            jnp.where((kind_r & 1) == 1, _log_sigmoid(gr), gr))


def _store_gate_cumsums(gc, gr, gc_ref, gr_ref, chunk):
    ri = lax.broadcasted_iota(jnp.int32, (chunk, chunk), 0)
    ci = lax.broadcasted_iota(jnp.int32, (chunk, chunk), 1)
    lower = jnp.where(ci <= ri, 1.0, 0.0).astype(BF16)
    upper = jnp.where(ci >= ri, 1.0, 0.0).astype(BF16)
    both = jnp.concatenate([lower, upper], axis=0)
    kc = lax.broadcasted_iota(jnp.int32, (chunk, N_GATES), 1) >> _log2(ML_HEADS)
    kr = lax.broadcasted_iota(jnp.int32, (N_GATES, chunk), 0) >> _log2(ML_HEADS)
    for c in range(gc.shape[0] // chunk):
        rows = slice(c * chunk, (c + 1) * chunk)
        pieces = _split3(gr[:, rows])
        cs = sum(_dot_nt(both, piece) for piece in pieces)
        gc_ref[rows, :] = jnp.where(kc == 1, cs[:chunk], jnp.where(kc == 3, cs[chunk:], gc[rows]))
        pre_r = sum(_dot(piece, upper) for piece in pieces)
        suf_r = sum(_dot(piece, lower) for piece in pieces)
        gr_ref[:, rows] = jnp.where(kr == 1, pre_r, jnp.where(kr == 3, suf_r, gr[:, rows]))


def _meta_proj_kernel(x_ref, g_ref, wna_ref, wqk_ref, wvzo_ref, wmg_ref, wg_ref, wgt_ref,
                      bna_ref, bqk_ref, bvzo_ref, bmg_ref, bg_ref, bgt_ref,
                      na_ref, qk_ref, vzo_ref, gc_ref, gr_ref):
    xb = _rmsnorm_bf16(x_ref[...], g_ref[...])
    na_ref[...] = (_dot(xb, wna_ref[...]) + bna_ref[...]).astype(BF16)
    qk_ref[...] = _dot(xb, wqk_ref[...]) + bqk_ref[...]
    vzo_ref[...] = (_dot(xb, wvzo_ref[...]) + bvzo_ref[...]).astype(BF16)
    gc_ref[...], gr_ref[...] = _gates(xb, wg_ref, wgt_ref, bg_ref, bgt_ref)


def _meta_proj(meta, g_pre, wts):
    n = meta.shape[0]
    shapes = ((n, 4 * NA_WIDTH, BF16), (n, 2 * ML_WIDTH, F32), (n, 3 * ML_WIDTH, BF16),
              (n, N_GATES, F32), (N_GATES, n, F32))
    return pl.pallas_call(
        _meta_proj_kernel,
        grid=(1,),
        in_specs=[_resident(meta), _resident(g_pre)] + [_resident(a) for a in wts],
        out_specs=tuple(pl.BlockSpec((r, c), lambda i: (0, 0)) for r, c, _ in shapes),
        out_shape=tuple(jax.ShapeDtypeStruct((r, c), d) for r, c, d in shapes),
        compiler_params=_params(1),
        name="meta_proj",
    )(meta, g_pre, *wts)


def _na_row_start(r, n_rows):
    return jnp.clip(r - NA_WIN_ROWS // 2, 0, n_rows - NA_WIN_ROWS)


def _na_kernel(q_ref, k_ref, v_ref, z_ref, mk_ref, mv_ref, bias_ref, o_ref, *, n_rows, rows_per_step):
    n_keys = NA_WIN_ROWS * GRID_W
    scale = NA_DH ** -0.5
    lane = lax.broadcasted_iota(jnp.int32, (GRID_W, NA_PAIR), 1)
    lo = lane < NA_DH
    n_pairs = NA_WIDTH // NA_PAIR
    groups = [slice(g * NA_PAIR, (g + 1) * NA_PAIR) for g in range(n_pairs)]

    def window(i):
        r = pl.program_id(1) * rows_per_step + i
        r0 = _na_row_start(r, n_rows)
        return pl.multiple_of(r0 * GRID_W, GRID_W), r0 - r + NA_WIN_ROWS - 1

    def scores(i):
        start, _ = window(i)
        qrows = slice(i * GRID_W, (i + 1) * GRID_W)
        out = []
        for cols in groups:
            qf = q_ref[qrows, cols].astype(F32) * scale
            q2 = jnp.concatenate([jnp.where(lo, qf, 0.0), jnp.where(lo, 0.0, qf)], axis=0).astype(BF16)
            s = _dot_nt(q2, k_ref[pl.ds(start, n_keys), cols])
            sm = _dot_nt(q2, mk_ref[:, cols])
            out.append((s, sm))
        return out

    def softmax(i, sc):
        _, off = window(i)
        out = []
        for g, (s, sm) in enumerate(sc):
            bias = bias_ref[off, g]
            s = jnp.where(bias > 0.5 * NEG, s + bias, NEG)
            m = jnp.maximum(jnp.max(s, axis=-1, keepdims=True), jnp.max(sm, axis=-1, keepdims=True))
            p = jnp.exp(s - m)
            pm = jnp.exp(sm - m)
            denom = jnp.sum(p, axis=-1, keepdims=True) + jnp.sum(pm, axis=-1, keepdims=True)
            out.append((p.astype(BF16), pm.astype(BF16), denom))
        return out

    def values(i, probs):
        start, _ = window(i)
        qrows = slice(i * GRID_W, (i + 1) * GRID_W)
        for cols, (p, pm, denom) in zip(groups, probs):
            o2 = _dot(p, v_ref[pl.ds(start, n_keys), cols]) + _dot(pm, mv_ref[:, cols])
            o2 = o2 / denom
            o = jnp.where(lo, o2[0:GRID_W], o2[GRID_W:2 * GRID_W])
            o_ref[qrows, cols] = (o * _silu(z_ref[qrows, cols].astype(F32))).astype(BF16)

    sc = scores(0)
    for i in range(rows_per_step):
        nxt = scores(i + 1) if i + 1 < rows_per_step else None
        values(i, softmax(i, sc))
        sc = nxt


def _na_bias_table(rpb):
    c = np.arange(GRID_W)
    c0 = np.clip(c - NA_WIN_COLS // 2, 0, GRID_W - NA_WIN_COLS)
    kc = np.arange(GRID_W)
    valid = (kc[None, :] >= c0[:, None]) & (kc[None, :] < c0[:, None] + NA_WIN_COLS)
    dc = kc[None, :] - c[:, None] + NA_WIN_COLS - 1
    n_dc = 2 * NA_WIN_COLS - 1
    n_dr = 2 * NA_WIN_ROWS - 1
    onehot = ((dc[None] == np.arange(n_dc)[:, None, None]) & valid[None]).astype(np.float32)
    toe = jnp.dot(rpb.astype(F32).reshape(NA_HEADS * n_dr, n_dc), onehot.reshape(n_dc, GRID_W * GRID_W),
                  precision=lax.Precision.HIGHEST)
    toe = jnp.where(valid.reshape(1, GRID_W * GRID_W), toe, NEG)
    toe = toe.reshape(NA_HEADS, n_dr, GRID_W, GRID_W)
    tab = jnp.stack([toe[:, off:off + NA_WIN_ROWS] for off in range(NA_WIN_ROWS)])
    tab = jnp.transpose(tab, (0, 1, 3, 2, 4))
    return tab.reshape(NA_WIN_ROWS, NA_WIDTH // NA_PAIR, 2 * GRID_W, NA_WIN_ROWS * GRID_W)


def _na(na, meta_na, bias_tab, *, batch, seq):
    n_rows = seq // GRID_W
    rps = NA_ROWS_PER_STEP
    assert n_rows >= NA_WIN_ROWS and seq % GRID_W == 0 and n_rows % rps == 0
    steps = n_rows // rps
    qz = lambda col: pl.BlockSpec((rps * GRID_W, NA_WIDTH), lambda b, r: (b * steps + r, col))
    kv = lambda col: pl.BlockSpec((seq, NA_WIDTH), lambda b, r: (b, col), pipeline_mode=pl.Buffered(1))
    meta = lambda col: pl.BlockSpec((N_META, NA_WIDTH), lambda b, r: (0, col), pipeline_mode=pl.Buffered(1))
    return pl.pallas_call(
        functools.partial(_na_kernel, n_rows=n_rows, rows_per_step=rps),
        grid=(batch, steps),
        in_specs=[qz(0), kv(1), kv(2), qz(3), meta(1), meta(2), _resident(bias_tab)],
        out_specs=qz(0),
        out_shape=jax.ShapeDtypeStruct((batch * seq, NA_WIDTH), BF16),
        compiler_params=_params(2),
        name="na",
    )(na, na, na, na, meta_na, meta_na, bias_tab)


def _conv_silu(x, prev8, next8, w, xs_ref):
    tb = x.shape[0]
    xs_ref[0:8, :] = prev8
    xs_ref[8:tb + 8, :] = x
    xs_ref[tb + 8:tb + 16, :] = next8
    y = (w[0:1] * xs_ref[7:tb + 7, :] + w[1:2] * xs_ref[8:tb + 8, :]
         + w[2:3] * xs_ref[9:tb + 9, :])
    return _silu(y)


def _twice(row):
    return jnp.concatenate([row, row], axis=1)


def _ml_block(q_all, k_all, v_ref, gc_ref, gr_ref, ct_ref, m_ref, emit, *, reverse, fillers=()):
    tb = q_all.shape[0]
    ch = ML_CHUNK
    n_chunks = tb // ch
    order = range(n_chunks - 1, -1, -1) if reverse else range(n_chunks)
    ig_i, b_i = (2 * ML_HEADS, 3 * ML_HEADS) if reverse else (0, ML_HEADS)
    ones = jnp.ones((ch, ML_DH), BF16)
    ri = lax.broadcasted_iota(jnp.int32, (ch, ch), 0)
    ci = lax.broadcasted_iota(jnp.int32, (ch, ch), 1)
    causal = (ci >= ri) if reverse else (ci <= ri)
    eye = ci == ri
    heads = range(ML_HEADS)
    cts = [ct_ref[hd] for hd in heads]
    ms = [m_ref[hd][0:1, :] for hd in heads]
    work = []
    for c in order:
        rows = slice(c * ch, (c + 1) * ch)
        for hd in heads:
            cols = slice(hd * ML_DH, (hd + 1) * ML_DH)
            bc = jnp.broadcast_to(gc_ref[rows, b_i + hd:b_i + hd + 1], (ch, ch))
            r = gr_ref[ig_i + hd:ig_i + hd + 1, rows] - gr_ref[b_i + hd:b_i + hd + 1, rows]
            g = bc[0:1] if reverse else bc[ch - 1:ch]
            logd = jnp.where(causal, bc + jnp.broadcast_to(r, (ch, ch)), NEG)
            rowmax = jnp.max(logd, axis=1, keepdims=True)
            a = g + r
            m = ms[hd]
            m_new = jnp.maximum(g + m, jnp.max(a, axis=1, keepdims=True))
            ms[hd] = m_new
            w_diag = jnp.where(eye, jnp.exp(a - m_new), 0.0).astype(BF16)
            work.append(dict(hd=hd, rows=rows, cols=cols, bc=bc, logd=logd, rowmax=rowmax, m=m,
                             fs=jnp.exp(g + m - m_new), w_diag=w_diag,
                             k=k_all[rows, cols].astype(BF16), q=q_all[rows, cols].astype(BF16),
                             v_ext=jnp.concatenate([v_ref[rows, cols], ones], axis=1)))
    if len(fillers) > 0:
        fillers[0]()
    for w in work:
        w["wv"] = _dot(w["w_diag"], w["v_ext"]).astype(BF16)
    for w in work:
        w["qk"] = _dot_nt(w["q"], w["k"])
    for w in work:
        w["dct"] = _dot_tn(w["k"], w["wv"])
    for w in work:
        w["r2"] = _dot(w["q"], cts[w["hd"]].astype(BF16))
        cts[w["hd"]] = _twice(w["fs"]) * cts[w["hd"]] + w["dct"]
    if len(fillers) > 1:
        fillers[1]()
    for w in work:
        inter = w["bc"] + w["m"]
        m_t = jnp.maximum(w["rowmax"], inter)
        w["e_inter"] = jnp.exp(inter - m_t)
        w["floor"] = jnp.exp(-m_t)
        s = (w["qk"] * jnp.exp(w["logd"] - m_t)).astype(BF16)
        w["r1"] = _dot(s, w["v_ext"])
    for w in work:
        num = w["r1"][:, :ML_DH] + w["e_inter"] * w["r2"][:, :ML_DH]
        den = w["r1"][:, ML_DH:] + w["e_inter"] * w["r2"][:, ML_DH:]
        emit(w["hd"], w["rows"], w["cols"], num / jnp.maximum(jnp.abs(den), w["floor"]))
    for hd in heads:
        ct_ref[hd] = cts[hd]
        m_ref[hd] = jnp.broadcast_to(ms[hd], m_ref.shape[1:])


def _proj_ml_fwd_kernel(x_ref, xnext_ref, g_ref, wna_ref, wqk_ref, wvzo_ref, wmg_ref, wg_ref, wgt_ref,
                        bna_ref, bqk_ref, bvzo_ref, bmg_ref, bg_ref, bgt_ref,
                        mqk_ref, mv_ref, mgc_ref, mgr_ref, cw_ref,
                        na_ref, vzo_ref, mg_ref, gc_ref, gr_ref, hf_ref, qc_ref, kc_ref,
                        ct_ref, m_ref, xs_ref, tail_ref, *, n_blocks):
    i = pl.program_id(1)
    tb = x_ref.shape[0]
    first = i == 0
    last = i == n_blocks - 1

    @pl.when(first)
    def _():
        w = cw_ref[:, ML_WIDTH:2 * ML_WIDTH]
        xs_ref[0:8, :] = jnp.zeros((8, ML_WIDTH), F32)
        xs_ref[8:8 + N_META, :] = mqk_ref[:, ML_WIDTH:2 * ML_WIDTH]
        kcols = slice(ML_WIDTH, 2 * ML_WIDTH)
        head = _dot(_rmsnorm_bf16(x_ref[0:16, :], g_ref[...]), wqk_ref[:, kcols]) + bqk_ref[:, kcols]
        xs_ref[8 + N_META:16 + N_META, :] = head[0:8]
        y = (w[0:1] * xs_ref[7:7 + N_META, :] + w[1:2] * xs_ref[8:8 + N_META, :]
             + w[2:3] * xs_ref[9:9 + N_META, :])
        mk = _silu(y) * (ML_DH ** -0.5)
        ones = jnp.ones((N_META, ML_DH), BF16)
        ri = lax.broadcasted_iota(jnp.int32, (N_META, N_META), 0)
        ci = lax.broadcasted_iota(jnp.int32, (N_META, N_META), 1)
        for hd in range(ML_HEADS):
            cols = slice(hd * ML_DH, (hd + 1) * ML_DH)
            ig = mgc_ref[:, hd:hd + 1]
            lf = mgr_ref[ML_HEADS + hd:ML_HEADS + hd + 1, :]
            a = jnp.sum(jnp.where(ci > ri, lf, 0.0), axis=1, keepdims=True) + ig
            g = jnp.sum(lf, axis=1, keepdims=True)
            m_loc = jnp.max(a, axis=0, keepdims=True)
            m0 = jnp.maximum(g, m_loc)
            v_ext = jnp.concatenate([mv_ref[:, cols], ones], axis=1)
            dct = _dot_tn((jnp.exp(a - m_loc) * mk[:, cols]).astype(BF16), v_ext)
            ct_ref[hd] = jnp.exp(m_loc - m0) * dct
            m_ref[hd] = jnp.broadcast_to(m0, m_ref.shape[1:])
        tail_ref[...] = mqk_ref[8:16, :]

    xb_ext = _rmsnorm_bf16(jnp.concatenate([x_ref[...], xnext_ref[...]], axis=0), g_ref[...])
    xb = xb_ext[0:tb]
    qk_ext = _dot(xb_ext, wqk_ref[...]) + bqk_ref[...]
    vzo_ref[...] = (_dot(xb, wvzo_ref[...]) + bvzo_ref[...]).astype(BF16)
    gc, gr = _gates(xb, wg_ref, wgt_ref, bg_ref, bgt_ref)
    _store_gate_cumsums(gc, gr, gc_ref, gr_ref, ML_CHUNK)
    qp, kp = qk_ext[:, 0:ML_WIDTH], qk_ext[:, ML_WIDTH:2 * ML_WIDTH]

    prev8 = tail_ref[...]
    zeros8 = jnp.zeros((8, ML_WIDTH), F32)
    qnext = jnp.where(last, zeros8, qp[tb:tb + 8])
    knext = jnp.where(last, zeros8, kp[tb:tb + 8])
    q_all = _conv_silu(qp[0:tb], prev8[:, 0:ML_WIDTH], qnext, cw_ref[:, 0:ML_WIDTH], xs_ref).astype(BF16)
    k_all = _conv_silu(kp[0:tb], prev8[:, ML_WIDTH:2 * ML_WIDTH], knext,
                       cw_ref[:, ML_WIDTH:2 * ML_WIDTH], xs_ref) * (ML_DH ** -0.5)
    tail_ref[...] = qk_ext[tb - 8:tb]
    qc_ref[...] = q_all
    kc_ref[...] = k_all.astype(BF16)

    def emit(hd, rows, cols, h):
        hf_ref[rows, cols] = h

    def project_na():
        na_ref[...] = (_dot(xb, wna_ref[...]) + bna_ref[...]).astype(BF16)

    def project_mg():
        mg_ref[...] = (_dot(xb, wmg_ref[...]) + bmg_ref[...]).astype(BF16)

    _ml_block(q_all, k_all, vzo_ref, gc_ref, gr_ref, ct_ref, m_ref, emit, reverse=False,
              fillers=(project_na, project_mg))


def _merge_project(x, ya_in, yb_in, mg_ref, wa_ref, wb_ref, wo_ref, g_ref):
    ya = _dot(ya_in, wa_ref[...])
    yb = _dot(yb_in, wb_ref[...])
    ga = _sigmoid(mg_ref[:, 0:D_MODEL].astype(F32))
    gb = _sigmoid(mg_ref[:, D_MODEL:2 * D_MODEL].astype(F32))
    y = _dot((ga * ya + gb * yb).astype(BF16), wo_ref[...])
    return x + y * lax.rsqrt(jnp.mean(y * y, axis=-1, keepdims=True) + EPS) * g_ref[...]


def _ml_bwd_out_kernel(qc_ref, kc_ref, v_ref, gc_ref, gr_ref, hf_ref, o_ref, z_ref, hg_ref,
                       x_ref, ya_ref, mg_ref, wa_ref, wb_ref, wo_ref, g_ref, y_ref,
                       ct_ref, m_ref, yb_ref):
    @pl.when(pl.program_id(1) == 0)
    def _():
        ct_ref[...] = jnp.zeros(ct_ref.shape, F32)
        m_ref[...] = jnp.zeros(m_ref.shape, F32)
        yb_ref[...] = jnp.zeros(yb_ref.shape, BF16)

    y_ref[...] = _merge_project(x_ref[...], ya_ref[...], yb_ref[...], mg_ref, wa_ref, wb_ref, wo_ref, g_ref)

    def emit(hd, rows, cols, h):
        hh = _sigmoid(o_ref[rows, cols].astype(F32)) * (hf_ref[rows, cols] + h)
        mu = jnp.mean(hh, axis=-1, keepdims=True)
        d = hh - mu
        var = jnp.mean(d * d, axis=-1, keepdims=True)
        y = d * lax.rsqrt(var + EPS) * hg_ref[:, cols] * _silu(z_ref[rows, cols].astype(F32))
        yb_ref[rows, cols] = y.astype(BF16)

    _ml_block(qc_ref[...], kc_ref[...], v_ref, gc_ref, gr_ref, ct_ref, m_ref, emit, reverse=True)


def _ml_specs(batch, seq, tb, blk):
    nb = seq // tb

    def main(col):
        return pl.BlockSpec((tb, ML_WIDTH), lambda b, j: (b * nb + blk(b, j), col))

    gc = pl.BlockSpec((tb, N_GATES), lambda b, j: (b * nb + blk(b, j), 0))
    gr = pl.BlockSpec((N_GATES, tb), lambda b, j: (0, b * nb + blk(b, j)))
    return main, gc, gr


def _ml_state_scratch():
    return [pltpu.VMEM((ML_HEADS, ML_DH, 2 * ML_DH), F32), pltpu.VMEM((ML_HEADS, 8, 128), F32)]


def _proj_ml_fwd(x2d, g_pre, wts, meta_qk, meta_vzo, meta_gc, meta_gr, conv_w, *, batch, seq, tb):
    nb = seq // tb
    halo = 16
    rows = batch * seq
    blk = lambda b, i: b * nb + i
    row = lambda width: pl.BlockSpec((tb, width), lambda b, i: (blk(b, i), 0))
    nxt = pl.BlockSpec((halo, D_MODEL),
                       lambda b, i: (jnp.minimum((blk(b, i) + 1) * (tb // halo), rows // halo - 1), 0))
    out_shape = (
        jax.ShapeDtypeStruct((rows, 4 * NA_WIDTH), BF16),
        jax.ShapeDtypeStruct((rows, 3 * ML_WIDTH), BF16),
        jax.ShapeDtypeStruct((rows, 2 * D_MODEL), BF16),
        jax.ShapeDtypeStruct((rows, N_GATES), F32),
        jax.ShapeDtypeStruct((N_GATES, rows), F32),
        jax.ShapeDtypeStruct((rows, ML_WIDTH), F32),
        jax.ShapeDtypeStruct((rows, ML_WIDTH), BF16),
        jax.ShapeDtypeStruct((rows, ML_WIDTH), BF16),
    )
    out_specs = (row(4 * NA_WIDTH), row(3 * ML_WIDTH), row(2 * D_MODEL), row(N_GATES),
                 pl.BlockSpec((N_GATES, tb), lambda b, i: (0, blk(b, i))),
                 row(ML_WIDTH), row(ML_WIDTH), row(ML_WIDTH))
    return pl.pallas_call(
        functools.partial(_proj_ml_fwd_kernel, n_blocks=nb),
        grid=(batch, nb),
        in_specs=[row(D_MODEL), nxt, _resident(g_pre)] + [_resident(a) for a in wts]
                 + [_resident(meta_qk), pl.BlockSpec((N_META, ML_WIDTH), lambda b, i: (0, 0)),
                    _resident(meta_gc), _resident(meta_gr), _resident(conv_w)],
        out_specs=out_specs,
        out_shape=out_shape,
        scratch_shapes=_ml_state_scratch() + [pltpu.VMEM((tb + 16, ML_WIDTH), F32),
                                              pltpu.VMEM((8, 2 * ML_WIDTH), F32)],
        compiler_params=_params(2),
        name="proj_ml_fwd",
    )(x2d, x2d, g_pre, *wts, meta_qk, meta_vzo, meta_gc, meta_gr, conv_w)


def _ml_bwd_out(qc, kc, vzo, gc, gr, hf, head_g, x2d, ya, mg, w_a, w_b, w_out, g_post, *, batch, seq, tb):
    nb = seq // tb
    main, gc_spec, gr_spec = _ml_specs(batch, seq, tb, lambda b, j: nb - 1 - jnp.minimum(j, nb - 1))
    wide = lambda width: pl.BlockSpec((tb, width), lambda b, j: (b * nb + nb - jnp.maximum(j, 1), 0))
    return pl.pallas_call(
        _ml_bwd_out_kernel,
        grid=(batch, nb + 1),
        in_specs=[main(0), main(0), main(0), gc_spec, gr_spec, main(0), main(2), main(1), _resident(head_g),
                  wide(D_MODEL), wide(NA_WIDTH), wide(2 * D_MODEL),
                  _resident(w_a), _resident(w_b), _resident(w_out), _resident(g_post)],
        out_specs=wide(D_MODEL),
        out_shape=jax.ShapeDtypeStruct((batch * seq, D_MODEL), F32),
        scratch_shapes=_ml_state_scratch() + [pltpu.VMEM((tb, ML_WIDTH), BF16)],
        compiler_params=_params(2),
        name="ml_bwd_out",
    )(qc, kc, vzo, gc, gr, hf, vzo, vzo, head_g, x2d, ya, mg, w_a, w_b, w_out, g_post)


def _split_in_proj(w_in, b_in):
    o_qk = 4 * NA_WIDTH
    o_vzo = o_qk + 2 * ML_WIDTH
    o_g = o_vzo + 3 * ML_WIDTH
    o_mg = o_g + N_GATES
    wb = w_in.astype(BF16)
    b = b_in.astype(F32)[None, :]
    wg = wb[:, o_g:o_mg]
    bg = b[:, o_g:o_mg]
    return (wb[:, :o_qk], wb[:, o_qk:o_vzo], wb[:, o_vzo:o_g], wb[:, o_mg:], wg, wg.T,
            b[:, :o_qk], b[:, o_qk:o_vzo], b[:, o_vzo:o_g], b[:, o_mg:], bg, bg.T)


def _trunk(x, meta_proj, wts, bias_tab, g_pre, conv_w, head_g, w_a, w_b, w_out, g_post):
    batch, seq, _ = x.shape
    assert seq % ML_BLOCK == 0 and ML_BLOCK % ML_CHUNK == 0
    x2d = x.reshape(batch * seq, D_MODEL)
    m_na, m_qk, m_vzo, m_gc, m_gr = meta_proj
    na, vzo, mg, gc, gr, hf, qc, kc = _proj_ml_fwd(x2d, g_pre, wts, m_qk, m_vzo, m_gc, m_gr, conv_w,
                                                   batch=batch, seq=seq, tb=ML_BLOCK)
    ya = _na(na, m_na, bias_tab, batch=batch, seq=seq)
    y = _ml_bwd_out(qc, kc, vzo, gc, gr, hf, head_g, x2d, ya, mg, w_a, w_b, w_out, g_post,
                    batch=batch, seq=seq, tb=ML_BLOCK)
    return y.reshape(batch, seq, D_MODEL)


def kernel(x_prompt, x_sample, meta_tokens, g_pre, w_in, b_in, na_rpb, ml_conv_w, ml_head_g, w_a, w_b, w_out, g_post):
    assert w_in.shape[0] == 1, "the meta-token outputs are only droppable for a single layer"
    wts = _split_in_proj(w_in[0], b_in[0])
    gp = g_pre[0].astype(F32)[None, :]
    meta_proj = _meta_proj(meta_tokens.astype(F32), gp, wts)
    bias_tab = _na_bias_table(na_rpb[0])
    args = (meta_proj, wts, bias_tab, gp, ml_conv_w[0].astype(F32), ml_head_g[0].astype(F32)[None, :],
            w_a[0].astype(BF16), w_b[0].astype(BF16), w_out[0].astype(BF16), g_post[0].astype(F32)[None, :])
    return (_trunk(x_prompt, *args), _trunk(x_sample, *args))
```

```python
import functools

import jax
import jax.numpy as jnp
import numpy as np
from jax import lax
from jax.experimental import pallas as pl
from jax.experimental.pallas import tpu as pltpu

D_MODEL = 1024
N_META = 16
GRID_W = 64
EPS = 1e-6
NEG = -1e30
NA_HEADS = 8
NA_DH = 64
NA_WIDTH = NA_HEADS * NA_DH
NA_PAIR = 2 * NA_DH
NA_WIN_ROWS = 8
NA_WIN_COLS = 16
NA_ROWS_PER_STEP = 8
ML_HEADS = 4
ML_DH = 128
ML_WIDTH = ML_HEADS * ML_DH
N_GATES = 4 * ML_HEADS
ML_CHUNK = 128
ML_BLOCK = 512
VMEM_LIMIT_BYTES = 56 * 1024 * 1024

F32 = jnp.float32
BF16 = jnp.bfloat16


def _dot(a, b):
    return jnp.dot(a, b, preferred_element_type=F32)


def _dot_nt(a, b):
    return lax.dot_general(a, b, (((1,), (1,)), ((), ())), preferred_element_type=F32)


def _dot_tn(a, b):
    return lax.dot_general(a, b, (((0,), (0,)), ((), ())), preferred_element_type=F32)


def _split3(x):
    hi = x.astype(BF16)
    r1 = x - hi.astype(F32)
    mid = r1.astype(BF16)
    lo = (r1 - mid.astype(F32)).astype(BF16)
    return hi, mid, lo


def _sigmoid(x):
    return 0.5 * jnp.tanh(0.5 * x) + 0.5


def _silu(x):
    return x * _sigmoid(x)


def _log_sigmoid(x):
    return jnp.minimum(x, 0.0) - jnp.log1p(jnp.exp(-jnp.abs(x)))


def _resident(a):
    return pl.BlockSpec(a.shape, lambda *_: (0,) * a.ndim, pipeline_mode=pl.Buffered(1))


def _log2(n):
    assert n & (n - 1) == 0
    return n.bit_length() - 1


def _params(n_grid_axes):
    return pltpu.CompilerParams(dimension_semantics=("arbitrary",) * n_grid_axes,
                                vmem_limit_bytes=VMEM_LIMIT_BYTES)


def _rmsnorm_bf16(x, g):
    return (x * lax.rsqrt(jnp.mean(x * x, axis=-1, keepdims=True) + EPS) * g).astype(BF16)


def _gates(xb, wg_ref, wgt_ref, bg_ref, bgt_ref):
    gc = _dot(xb, wg_ref[...]) + bg_ref[...]
    gr = _dot_nt(wgt_ref[...], xb) + bgt_ref[...]
    kind_c = lax.broadcasted_iota(jnp.int32, gc.shape, 1) >> _log2(ML_HEADS)
    kind_r = lax.broadcasted_iota(jnp.int32, gr.shape, 0) >> _log2(ML_HEADS)
    return (jnp.where((kind_c & 1) == 1, _log_sigmoid(gc), gc),
            jnp.where((kind_r & 1) == 1, _log_sigmoid(gr), gr))


def _store_gate_cumsums(gc, gr, gc_ref, gr_ref, chunk):
    ri = lax.broadcasted_iota(jnp.int32, (chunk, chunk), 0)
    ci = lax.broadcasted_iota(jnp.int32, (chunk, chunk), 1)
    lower = jnp.where(ci <= ri, 1.0, 0.0).astype(BF16)
    upper = jnp.where(ci >= ri, 1.0, 0.0).astype(BF16)
    both = jnp.concatenate([lower, upper], axis=0)
    kc = lax.broadcasted_iota(jnp.int32, (chunk, N_GATES), 1) >> _log2(ML_HEADS)
    kr = lax.broadcasted_iota(jnp.int32, (N_GATES, chunk), 0) >> _log2(ML_HEADS)
    for c in range(gc.shape[0] // chunk):
        rows = slice(c * chunk, (c + 1) * chunk)
        pieces = _split3(gr[:, rows])
        cs = sum(_dot_nt(both, piece) for piece in pieces)
        gc_ref[rows, :] = jnp.where(kc == 1, cs[:chunk], jnp.where(kc == 3, cs[chunk:], gc[rows]))
        pre_r = sum(_dot(piece, upper) for piece in pieces)
        suf_r = sum(_dot(piece, lower) for piece in pieces)
        gr_ref[:, rows] = jnp.where(kr == 1, pre_r, jnp.where(kr == 3, suf_r, gr[:, rows]))


def _meta_proj_kernel(x_ref, g_ref, wna_ref, wqk_ref, wvzo_ref, wmg_ref, wg_ref, wgt_ref,
                      bna_ref, bqk_ref, bvzo_ref, bmg_ref, bg_ref, bgt_ref,
                      na_ref, qk_ref, vzo_ref, gc_ref, gr_ref):
    xb = _rmsnorm_bf16(x_ref[...], g_ref[...])
    na_ref[...] = (_dot(xb, wna_ref[...]) + bna_ref[...]).astype(BF16)
    qk_ref[...] = _dot(xb, wqk_ref[...]) + bqk_ref[...]
    vzo_ref[...] = (_dot(xb, wvzo_ref[...]) + bvzo_ref[...]).astype(BF16)
    gc_ref[...], gr_ref[...] = _gates(xb, wg_ref, wgt_ref, bg_ref, bgt_ref)


def _meta_proj(meta, g_pre, wts):
    n = meta.shape[0]
    shapes = ((n, 4 * NA_WIDTH, BF16), (n, 2 * ML_WIDTH, F32), (n, 3 * ML_WIDTH, BF16),
              (n, N_GATES, F32), (N_GATES, n, F32))
    return pl.pallas_call(
        _meta_proj_kernel,
        grid=(1,),
        in_specs=[_resident(meta), _resident(g_pre)] + [_resident(a) for a in wts],
        out_specs=tuple(pl.BlockSpec((r, c), lambda i: (0, 0)) for r, c, _ in shapes),
        out_shape=tuple(jax.ShapeDtypeStruct((r, c), d) for r, c, d in shapes),
        compiler_params=_params(1),
        name="meta_proj",
    )(meta, g_pre, *wts)


def _na_row_start(r, n_rows):
    return jnp.clip(r - NA_WIN_ROWS // 2, 0, n_rows - NA_WIN_ROWS)


def _na_kernel(q_ref, k_ref, v_ref, z_ref, mk_ref, mv_ref, bias_ref, o_ref, *, n_rows, rows_per_step):
    n_keys = NA_WIN_ROWS * GRID_W
    scale = NA_DH ** -0.5
    lane = lax.broadcasted_iota(jnp.int32, (GRID_W, NA_PAIR), 1)
    lo = lane < NA_DH
    n_pairs = NA_WIDTH // NA_PAIR
    groups = [slice(g * NA_PAIR, (g + 1) * NA_PAIR) for g in range(n_pairs)]

    def window(i):
        r = pl.program_id(1) * rows_per_step + i
        r0 = _na_row_start(r, n_rows)
        return pl.multiple_of(r0 * GRID_W, GRID_W), r0 - r + NA_WIN_ROWS - 1

    def scores(i):
        start, _ = window(i)
        qrows = slice(i * GRID_W, (i + 1) * GRID_W)
        out = []
        for cols in groups:
            qf = q_ref[qrows, cols].astype(F32) * scale
            q2 = jnp.concatenate([jnp.where(lo, qf, 0.0), jnp.where(lo, 0.0, qf)], axis=0).astype(BF16)
            s = _dot_nt(q2, k_ref[pl.ds(start, n_keys), cols])
            sm = _dot_nt(q2, mk_ref[:, cols])
            out.append((s, sm))
        return out

    def softmax(i, sc):
        _, off = window(i)
        out = []
        for g, (s, sm) in enumerate(sc):
            bias = bias_ref[off, g]
            s = jnp.where(bias > 0.5 * NEG, s + bias, NEG)
            m = jnp.maximum(jnp.max(s, axis=-1, keepdims=True), jnp.max(sm, axis=-1, keepdims=True))
            p = jnp.exp(s - m)
            pm = jnp.exp(sm - m)
            denom = jnp.sum(p, axis=-1, keepdims=True) + jnp.sum(pm, axis=-1, keepdims=True)
            out.append((p.astype(BF16), pm.astype(BF16), denom))
        return out

    def values(i, probs):
        start, _ = window(i)
        qrows = slice(i * GRID_W, (i + 1) * GRID_W)
        for cols, (p, pm, denom) in zip(groups, probs):
            o2 = _dot(p, v_ref[pl.ds(start, n_keys), cols]) + _dot(pm, mv_ref[:, cols])
            o2 = o2 / denom
            o = jnp.where(lo, o2[0:GRID_W], o2[GRID_W:2 * GRID_W])
            o_ref[qrows, cols] = (o * _silu(z_ref[qrows, cols].astype(F32))).astype(BF16)

    sc = scores(0)
    for i in range(rows_per_step):
        nxt = scores(i + 1) if i + 1 < rows_per_step else None
        values(i, softmax(i, sc))
        sc = nxt


def _na_bias_table(rpb):
    c = np.arange(GRID_W)
    c0 = np.clip(c - NA_WIN_COLS // 2, 0, GRID_W - NA_WIN_COLS)
    kc = np.arange(GRID_W)
    valid = (kc[None, :] >= c0[:, None]) & (kc[None, :] < c0[:, None] + NA_WIN_COLS)
    dc = kc[None, :] - c[:, None] + NA_WIN_COLS - 1
    n_dc = 2 * NA_WIN_COLS - 1
    n_dr = 2 * NA_WIN_ROWS - 1
    onehot = ((dc[None] == np.arange(n_dc)[:, None, None]) & valid[None]).astype(np.float32)
    toe = jnp.dot(rpb.astype(F32).reshape(NA_HEADS * n_dr, n_dc), onehot.reshape(n_dc, GRID_W * GRID_W),
                  precision=lax.Precision.HIGHEST)
    toe = jnp.where(valid.reshape(1, GRID_W * GRID_W), toe, NEG)
    toe = toe.reshape(NA_HEADS, n_dr, GRID_W, GRID_W)
    tab = jnp.stack([toe[:, off:off + NA_WIN_ROWS] for off in range(NA_WIN_ROWS)])
    tab = jnp.transpose(tab, (0, 1, 3, 2, 4))
    return tab.reshape(NA_WIN_ROWS, NA_WIDTH // NA_PAIR, 2 * GRID_W, NA_WIN_ROWS * GRID_W)


def _na(na, meta_na, bias_tab, *, batch, seq):
    n_rows = seq // GRID_W
    rps = NA_ROWS_PER_STEP
    assert n_rows >= NA_WIN_ROWS and seq % GRID_W == 0 and n_rows % rps == 0
    steps = n_rows // rps
    qz = lambda col: pl.BlockSpec((rps * GRID_W, NA_WIDTH), lambda b, r: (b * steps + r, col))
    kv = lambda col: pl.BlockSpec((seq, NA_WIDTH), lambda b, r: (b, col))
    meta = lambda col: pl.BlockSpec((N_META, NA_WIDTH), lambda b, r: (0, col), pipeline_mode=pl.Buffered(1))
    return pl.pallas_call(
        functools.partial(_na_kernel, n_rows=n_rows, rows_per_step=rps),
        grid=(batch, steps),
        in_specs=[qz(0), kv(1), kv(2), qz(3), meta(1), meta(2), _resident(bias_tab)],
        out_specs=qz(0),
        out_shape=jax.ShapeDtypeStruct((batch * seq, NA_WIDTH), BF16),
        compiler_params=_params(2),
        name="na",
    )(na, na, na, na, meta_na, meta_na, bias_tab)


def _conv_silu(x, prev8, next8, w, xs_ref):
    tb = x.shape[0]
    xs_ref[0:8, :] = prev8
    xs_ref[8:tb + 8, :] = x
    xs_ref[tb + 8:tb + 16, :] = next8
    y = (w[0:1] * xs_ref[7:tb + 7, :] + w[1:2] * xs_ref[8:tb + 8, :]
         + w[2:3] * xs_ref[9:tb + 9, :])
    return _silu(y)


def _twice(row):
    return jnp.concatenate([row, row], axis=1)


def _ml_block(q_all, k_all, v_ref, gc_ref, gr_ref, ct_ref, m_ref, emit, *, reverse, fillers=()):
    tb = q_all.shape[0]
    ch = ML_CHUNK
    n_chunks = tb // ch
    order = range(n_chunks - 1, -1, -1) if reverse else range(n_chunks)
    ig_i, b_i = (2 * ML_HEADS, 3 * ML_HEADS) if reverse else (0, ML_HEADS)
    ones = jnp.ones((ch, ML_DH), BF16)
    ri = lax.broadcasted_iota(jnp.int32, (ch, ch), 0)
    ci = lax.broadcasted_iota(jnp.int32, (ch, ch), 1)
    causal = (ci >= ri) if reverse else (ci <= ri)
    eye = ci == ri
    heads = range(ML_HEADS)
    cts = [ct_ref[hd] for hd in heads]
    ms = [m_ref[hd][0:1, :] for hd in heads]
    work = []
    for c in order:
        rows = slice(c * ch, (c + 1) * ch)
        for hd in heads:
            cols = slice(hd * ML_DH, (hd + 1) * ML_DH)
            bc = jnp.broadcast_to(gc_ref[rows, b_i + hd:b_i + hd + 1], (ch, ch))
            r = gr_ref[ig_i + hd:ig_i + hd + 1, rows] - gr_ref[b_i + hd:b_i + hd + 1, rows]
            g = bc[0:1] if reverse else bc[ch - 1:ch]
            logd = jnp.where(causal, bc + jnp.broadcast_to(r, (ch, ch)), NEG)
            rowmax = jnp.max(logd, axis=1, keepdims=True)
            a = g + r
            m = ms[hd]
            m_new = jnp.maximum(g + m, jnp.max(a, axis=1, keepdims=True))
            ms[hd] = m_new
            w_diag = jnp.where(eye, jnp.exp(a - m_new), 0.0).astype(BF16)
            work.append(dict(hd=hd, rows=rows, cols=cols, bc=bc, logd=logd, rowmax=rowmax, m=m,
                             fs=jnp.exp(g + m - m_new), w_diag=w_diag,
                             k=k_all[rows, cols].astype(BF16), q=q_all[rows, cols].astype(BF16),
                             v_ext=jnp.concatenate([v_ref[rows, cols], ones], axis=1)))
    if len(fillers) > 0:
        fillers[0]()
    for w in work:
        w["wv"] = _dot(w["w_diag"], w["v_ext"]).astype(BF16)
    for w in work:
        w["qk"] = _dot_nt(w["q"], w["k"])
    for w in work:
        w["dct"] = _dot_tn(w["k"], w["wv"])
    for w in work:
        w["r2"] = _dot(w["q"], cts[w["hd"]].astype(BF16))
        cts[w["hd"]] = _twice(w["fs"]) * cts[w["hd"]] + w["dct"]
    if len(fillers) > 1:
        fillers[1]()
    for w in work:
        inter = w["bc"] + w["m"]
        m_t = jnp.maximum(w["rowmax"], inter)
        w["e_inter"] = jnp.exp(inter - m_t)
        w["floor"] = jnp.exp(-m_t)
        s = (w["qk"] * jnp.exp(w["logd"] - m_t)).astype(BF16)
        w["r1"] = _dot(s, w["v_ext"])
    for w in work:
        num = w["r1"][:, :ML_DH] + w["e_inter"] * w["r2"][:, :ML_DH]
        den = w["r1"][:, ML_DH:] + w["e_inter"] * w["r2"][:, ML_DH:]
        emit(w["hd"], w["rows"], w["cols"], num / jnp.maximum(jnp.abs(den), w["floor"]))
    for hd in heads:
        ct_ref[hd] = cts[hd]
        m_ref[hd] = jnp.broadcast_to(ms[hd], m_ref.shape[1:])


def _proj_ml_fwd_kernel(x_ref, xnext_ref, g_ref, wna_ref, wqk_ref, wvzo_ref, wmg_ref, wg_ref, wgt_ref,
                        bna_ref, bqk_ref, bvzo_ref, bmg_ref, bg_ref, bgt_ref,
                        mqk_ref, mv_ref, mgc_ref, mgr_ref, cw_ref,
                        na_ref, vzo_ref, mg_ref, gc_ref, gr_ref, hf_ref, qc_ref, kc_ref,
                        ct_ref, m_ref, xs_ref, tail_ref, *, n_blocks):
    i = pl.program_id(1)
    tb = x_ref.shape[0]
    first = i == 0
    last = i == n_blocks - 1

    @pl.when(first)
    def _():
        w = cw_ref[:, ML_WIDTH:2 * ML_WIDTH]
        xs_ref[0:8, :] = jnp.zeros((8, ML_WIDTH), F32)
        xs_ref[8:8 + N_META, :] = mqk_ref[:, ML_WIDTH:2 * ML_WIDTH]
        kcols = slice(ML_WIDTH, 2 * ML_WIDTH)
        head = _dot(_rmsnorm_bf16(x_ref[0:16, :], g_ref[...]), wqk_ref[:, kcols]) + bqk_ref[:, kcols]
        xs_ref[8 + N_META:16 + N_META, :] = head[0:8]
        y = (w[0:1] * xs_ref[7:7 + N_META, :] + w[1:2] * xs_ref[8:8 + N_META, :]
             + w[2:3] * xs_ref[9:9 + N_META, :])
        mk = _silu(y) * (ML_DH ** -0.5)
        ones = jnp.ones((N_META, ML_DH), BF16)
        ri = lax.broadcasted_iota(jnp.int32, (N_META, N_META), 0)
        ci = lax.broadcasted_iota(jnp.int32, (N_META, N_META), 1)
        for hd in range(ML_HEADS):
            cols = slice(hd * ML_DH, (hd + 1) * ML_DH)
            ig = mgc_ref[:, hd:hd + 1]
            lf = mgr_ref[ML_HEADS + hd:ML_HEADS + hd + 1, :]
            a = jnp.sum(jnp.where(ci > ri, lf, 0.0), axis=1, keepdims=True) + ig
            g = jnp.sum(lf, axis=1, keepdims=True)
            m_loc = jnp.max(a, axis=0, keepdims=True)
            m0 = jnp.maximum(g, m_loc)
            v_ext = jnp.concatenate([mv_ref[:, cols], ones], axis=1)
            dct = _dot_tn((jnp.exp(a - m_loc) * mk[:, cols]).astype(BF16), v_ext)
            ct_ref[hd] = jnp.exp(m_loc - m0) * dct
            m_ref[hd] = jnp.broadcast_to(m0, m_ref.shape[1:])
        tail_ref[...] = mqk_ref[8:16, :]

    xb_ext = _rmsnorm_bf16(jnp.concatenate([x_ref[...], xnext_ref[...]], axis=0), g_ref[...])
    xb = xb_ext[0:tb]
    qk_ext = _dot(xb_ext, wqk_ref[...]) + bqk_ref[...]
    vzo_ref[...] = (_dot(xb, wvzo_ref[...]) + bvzo_ref[...]).astype(BF16)
    gc, gr = _gates(xb, wg_ref, wgt_ref, bg_ref, bgt_ref)
    _store_gate_cumsums(gc, gr, gc_ref, gr_ref, ML_CHUNK)
    qp, kp = qk_ext[:, 0:ML_WIDTH], qk_ext[:, ML_WIDTH:2 * ML_WIDTH]

    prev8 = tail_ref[...]
    zeros8 = jnp.zeros((8, ML_WIDTH), F32)
    qnext = jnp.where(last, zeros8, qp[tb:tb + 8])
    knext = jnp.where(last, zeros8, kp[tb:tb + 8])
    q_all = _conv_silu(qp[0:tb], prev8[:, 0:ML_WIDTH], qnext, cw_ref[:, 0:ML_WIDTH], xs_ref).astype(BF16)
    k_all = _conv_silu(kp[0:tb], prev8[:, ML_WIDTH:2 * ML_WIDTH], knext,
                       cw_ref[:, ML_WIDTH:2 * ML_WIDTH], xs_ref) * (ML_DH ** -0.5)
    tail_ref[...] = qk_ext[tb - 8:tb]
    qc_ref[...] = q_all
    kc_ref[...] = k_all.astype(BF16)

    def emit(hd, rows, cols, h):
        hf_ref[rows, cols] = h

    def project_na():
        na_ref[...] = (_dot(xb, wna_ref[...]) + bna_ref[...]).astype(BF16)

    def project_mg():
        mg_ref[...] = (_dot(xb, wmg_ref[...]) + bmg_ref[...]).astype(BF16)

    _ml_block(q_all, k_all, vzo_ref, gc_ref, gr_ref, ct_ref, m_ref, emit, reverse=False,
              fillers=(project_na, project_mg))


def _merge_project(x, ya_in, yb_in, mg_ref, wa_ref, wb_ref, wo_ref, g_ref):
    ya = _dot(ya_in, wa_ref[...])
    yb = _dot(yb_in, wb_ref[...])
    ga = _sigmoid(mg_ref[:, 0:D_MODEL].astype(F32))
    gb = _sigmoid(mg_ref[:, D_MODEL:2 * D_MODEL].astype(F32))
    y = _dot((ga * ya + gb * yb).astype(BF16), wo_ref[...])
    return x + y * lax.rsqrt(jnp.mean(y * y, axis=-1, keepdims=True) + EPS) * g_ref[...]


def _ml_bwd_out_kernel(qc_ref, kc_ref, v_ref, gc_ref, gr_ref, hf_ref, o_ref, z_ref, hg_ref,
                       x_ref, ya_ref, mg_ref, wa_ref, wb_ref, wo_ref, g_ref, y_ref,
                       ct_ref, m_ref, yb_ref):
    @pl.when(pl.program_id(1) == 0)
    def _():
        ct_ref[...] = jnp.zeros(ct_ref.shape, F32)
        m_ref[...] = jnp.zeros(m_ref.shape, F32)
        yb_ref[...] = jnp.zeros(yb_ref.shape, BF16)

    y_ref[...] = _merge_project(x_ref[...], ya_ref[...], yb_ref[...], mg_ref, wa_ref, wb_ref, wo_ref, g_ref)

    def emit(hd, rows, cols, h):
        hh = _sigmoid(o_ref[rows, cols].astype(F32)) * (hf_ref[rows, cols] + h)
        mu = jnp.mean(hh, axis=-1, keepdims=True)
        d = hh - mu
        var = jnp.mean(d * d, axis=-1, keepdims=True)
        y = d * lax.rsqrt(var + EPS) * hg_ref[:, cols] * _silu(z_ref[rows, cols].astype(F32))
        yb_ref[rows, cols] = y.astype(BF16)

    _ml_block(qc_ref[...], kc_ref[...], v_ref, gc_ref, gr_ref, ct_ref, m_ref, emit, reverse=True)


def _ml_specs(batch, seq, tb, blk):
    nb = seq // tb

    def main(col):
        return pl.BlockSpec((tb, ML_WIDTH), lambda b, j: (b * nb + blk(b, j), col))

    gc = pl.BlockSpec((tb, N_GATES), lambda b, j: (b * nb + blk(b, j), 0))
    gr = pl.BlockSpec((N_GATES, tb), lambda b, j: (0, b * nb + blk(b, j)))
    return main, gc, gr


def _ml_state_scratch():
    return [pltpu.VMEM((ML_HEADS, ML_DH, 2 * ML_DH), F32), pltpu.VMEM((ML_HEADS, 8, 128), F32)]


def _proj_ml_fwd(x2d, g_pre, wts, meta_qk, meta_vzo, meta_gc, meta_gr, conv_w, *, batch, seq, tb):
    nb = seq // tb
    halo = 16
    rows = batch * seq
    blk = lambda b, i: b * nb + i
    row = lambda width: pl.BlockSpec((tb, width), lambda b, i: (blk(b, i), 0))
    nxt = pl.BlockSpec((halo, D_MODEL),
                       lambda b, i: (jnp.minimum((blk(b, i) + 1) * (tb // halo), rows // halo - 1), 0))
    out_shape = (
        jax.ShapeDtypeStruct((rows, 4 * NA_WIDTH), BF16),
        jax.ShapeDtypeStruct((rows, 3 * ML_WIDTH), BF16),
        jax.ShapeDtypeStruct((rows, 2 * D_MODEL), BF16),
        jax.ShapeDtypeStruct((rows, N_GATES), F32),
        jax.ShapeDtypeStruct((N_GATES, rows), F32),
        jax.ShapeDtypeStruct((rows, ML_WIDTH), F32),
        jax.ShapeDtypeStruct((rows, ML_WIDTH), BF16),
        jax.ShapeDtypeStruct((rows, ML_WIDTH), BF16),
    )
    out_specs = (row(4 * NA_WIDTH), row(3 * ML_WIDTH), row(2 * D_MODEL), row(N_GATES),
                 pl.BlockSpec((N_GATES, tb), lambda b, i: (0, blk(b, i))),
                 row(ML_WIDTH), row(ML_WIDTH), row(ML_WIDTH))
    return pl.pallas_call(
        functools.partial(_proj_ml_fwd_kernel, n_blocks=nb),
        grid=(batch, nb),
        in_specs=[row(D_MODEL), nxt, _resident(g_pre)] + [_resident(a) for a in wts]
                 + [_resident(meta_qk), pl.BlockSpec((N_META, ML_WIDTH), lambda b, i: (0, 0)),
                    _resident(meta_gc), _resident(meta_gr), _resident(conv_w)],
        out_specs=out_specs,
        out_shape=out_shape,
        scratch_shapes=_ml_state_scratch() + [pltpu.VMEM((tb + 16, ML_WIDTH), F32),
                                              pltpu.VMEM((8, 2 * ML_WIDTH), F32)],
        compiler_params=_params(2),
        name="proj_ml_fwd",
    )(x2d, x2d, g_pre, *wts, meta_qk, meta_vzo, meta_gc, meta_gr, conv_w)


def _ml_bwd_out(qc, kc, vzo, gc, gr, hf, head_g, x2d, ya, mg, w_a, w_b, w_out, g_post, *, batch, seq, tb):
    nb = seq // tb
    main, gc_spec, gr_spec = _ml_specs(batch, seq, tb, lambda b, j: nb - 1 - jnp.minimum(j, nb - 1))
    wide = lambda width: pl.BlockSpec((tb, width), lambda b, j: (b * nb + nb - jnp.maximum(j, 1), 0))
    return pl.pallas_call(
        _ml_bwd_out_kernel,
        grid=(batch, nb + 1),
        in_specs=[main(0), main(0), main(0), gc_spec, gr_spec, main(0), main(2), main(1), _resident(head_g),
                  wide(D_MODEL), wide(NA_WIDTH), wide(2 * D_MODEL),
                  _resident(w_a), _resident(w_b), _resident(w_out), _resident(g_post)],
        out_specs=wide(D_MODEL),
        out_shape=jax.ShapeDtypeStruct((batch * seq, D_MODEL), F32),
        scratch_shapes=_ml_state_scratch() + [pltpu.VMEM((tb, ML_WIDTH), BF16)],
        compiler_params=_params(2),
        name="ml_bwd_out",
    )(qc, kc, vzo, gc, gr, hf, vzo, vzo, head_g, x2d, ya, mg, w_a, w_b, w_out, g_post)


def _split_in_proj(w_in, b_in):
    o_qk = 4 * NA_WIDTH
    o_vzo = o_qk + 2 * ML_WIDTH
    o_g = o_vzo + 3 * ML_WIDTH
    o_mg = o_g + N_GATES
    wb = w_in.astype(BF16)
    b = b_in.astype(F32)[None, :]
    wg = wb[:, o_g:o_mg]
    bg = b[:, o_g:o_mg]
    return (wb[:, :o_qk], wb[:, o_qk:o_vzo], wb[:, o_vzo:o_g], wb[:, o_mg:], wg, wg.T,
            b[:, :o_qk], b[:, o_qk:o_vzo], b[:, o_vzo:o_g], b[:, o_mg:], bg, bg.T)


def _trunk(x, meta_proj, wts, bias_tab, g_pre, conv_w, head_g, w_a, w_b, w_out, g_post):
    batch, seq, _ = x.shape
    assert seq % ML_BLOCK == 0 and ML_BLOCK % ML_CHUNK == 0
    x2d = x.reshape(batch * seq, D_MODEL)
    m_na, m_qk, m_vzo, m_gc, m_gr = meta_proj
    na, vzo, mg, gc, gr, hf, qc, kc = _proj_ml_fwd(x2d, g_pre, wts, m_qk, m_vzo, m_gc, m_gr, conv_w,
                                                   batch=batch, seq=seq, tb=ML_BLOCK)
    ya = _na(na, m_na, bias_tab, batch=batch, seq=seq)
    y = _ml_bwd_out(qc, kc, vzo, gc, gr, hf, head_g, x2d, ya, mg, w_a, w_b, w_out, g_post,
                    batch=batch, seq=seq, tb=ML_BLOCK)
    return y.reshape(batch, seq, D_MODEL)


def kernel(x_prompt, x_sample, meta_tokens, g_pre, w_in, b_in, na_rpb, ml_conv_w, ml_head_g, w_a, w_b, w_out, g_post):
    assert w_in.shape[0] == 1, "the meta-token outputs are only droppable for a single layer"
    wts = _split_in_proj(w_in[0], b_in[0])
    gp = g_pre[0].astype(F32)[None, :]
    meta_proj = _meta_proj(meta_tokens.astype(F32), gp, wts)
    bias_tab = _na_bias_table(na_rpb[0])
    args = (meta_proj, wts, bias_tab, gp, ml_conv_w[0].astype(F32), ml_head_g[0].astype(F32)[None, :],
            w_a[0].astype(BF16), w_b[0].astype(BF16), w_out[0].astype(BF16), g_post[0].astype(F32)[None, :])
    return (_trunk(x_prompt, *args), _trunk(x_sample, *args))
```

```python
import functools

import jax
import jax.numpy as jnp
import numpy as np
from jax import lax
from jax.experimental import pallas as pl
from jax.experimental.pallas import tpu as pltpu

D_MODEL = 1024
N_META = 16
GRID_W = 64
EPS = 1e-6
NEG = -1e30
NA_HEADS = 8
NA_DH = 64
NA_WIDTH = NA_HEADS * NA_DH
NA_PAIR = 2 * NA_DH
NA_WIN_ROWS = 8
NA_WIN_COLS = 16
NA_ROWS_PER_STEP = 8
ML_HEADS = 4
ML_DH = 128
ML_WIDTH = ML_HEADS * ML_DH
N_GATES = 4 * ML_HEADS
ML_CHUNK = 128
FWD_BLOCK = 512
BWD_BLOCK = 512
VMEM_LIMIT_BYTES = 56 * 1024 * 1024

F32 = jnp.float32
BF16 = jnp.bfloat16


def _dot(a, b):
    return jnp.dot(a, b, preferred_element_type=F32)


def _dot_nt(a, b):
    return lax.dot_general(a, b, (((1,), (1,)), ((), ())), preferred_element_type=F32)


def _dot_tn(a, b):
    return lax.dot_general(a, b, (((0,), (0,)), ((), ())), preferred_element_type=F32)


def _split3(x):
    hi = x.astype(BF16)
    r1 = x - hi.astype(F32)
    mid = r1.astype(BF16)
    lo = (r1 - mid.astype(F32)).astype(BF16)
    return hi, mid, lo


def _sigmoid(x):
    return 0.5 * jnp.tanh(0.5 * x) + 0.5


def _silu(x):
    return x * _sigmoid(x)


def _log_sigmoid(x):
    return jnp.minimum(x, 0.0) - jnp.log1p(jnp.exp(-jnp.abs(x)))


def _resident(a):
    return pl.BlockSpec(a.shape, lambda *_: (0,) * a.ndim, pipeline_mode=pl.Buffered(1))


def _log2(n):
    assert n & (n - 1) == 0
    return n.bit_length() - 1


def _params(n_grid_axes):
    return pltpu.CompilerParams(dimension_semantics=("arbitrary",) * n_grid_axes,
                                vmem_limit_bytes=VMEM_LIMIT_BYTES)


def _rmsnorm_bf16(x, g):
    return (x * lax.rsqrt(jnp.mean(x * x, axis=-1, keepdims=True) + EPS) * g).astype(BF16)


def _gates(xb, wg_ref, wgt_ref, bg_ref, bgt_ref):
    gc = _dot(xb, wg_ref[...]) + bg_ref[...]
    gr = _dot_nt(wgt_ref[...], xb) + bgt_ref[...]
    kind_c = lax.broadcasted_iota(jnp.int32, gc.shape, 1) >> _log2(ML_HEADS)
    kind_r = lax.broadcasted_iota(jnp.int32, gr.shape, 0) >> _log2(ML_HEADS)
    return (jnp.where((kind_c & 1) == 1, _log_sigmoid(gc), gc),
            jnp.where((kind_r & 1) == 1, _log_sigmoid(gr), gr))


def _store_gate_cumsums(gc, gr, gc_ref, gr_ref, chunk):
    ri = lax.broadcasted_iota(jnp.int32, (chunk, chunk), 0)
    ci = lax.broadcasted_iota(jnp.int32, (chunk, chunk), 1)
    lower = jnp.where(ci <= ri, 1.0, 0.0).astype(BF16)
    upper = jnp.where(ci >= ri, 1.0, 0.0).astype(BF16)
    both = jnp.concatenate([lower, upper], axis=0)
    kc = lax.broadcasted_iota(jnp.int32, (chunk, N_GATES), 1) >> _log2(ML_HEADS)
    kr = lax.broadcasted_iota(jnp.int32, (N_GATES, chunk), 0) >> _log2(ML_HEADS)
    for c in range(gc.shape[0] // chunk):
        rows = slice(c * chunk, (c + 1) * chunk)
        pieces = _split3(gr[:, rows])
        cs = sum(_dot_nt(both, piece) for piece in pieces)
        gc_ref[rows, :] = jnp.where(kc == 1, cs[:chunk], jnp.where(kc == 3, cs[chunk:], gc[rows]))
        pre_r = sum(_dot(piece, upper) for piece in pieces)
        suf_r = sum(_dot(piece, lower) for piece in pieces)
        gr_ref[:, rows] = jnp.where(kr == 1, pre_r, jnp.where(kr == 3, suf_r, gr[:, rows]))


def _meta_proj_kernel(x_ref, g_ref, wna_ref, wqk_ref, wvzo_ref, wmg_ref, wg_ref, wgt_ref,
                      bna_ref, bqk_ref, bvzo_ref, bmg_ref, bg_ref, bgt_ref,
                      na_ref, qk_ref, vzo_ref, gc_ref, gr_ref):
    xb = _rmsnorm_bf16(x_ref[...], g_ref[...])
    na_ref[...] = (_dot(xb, wna_ref[...]) + bna_ref[...]).astype(BF16)
    qk_ref[...] = _dot(xb, wqk_ref[...]) + bqk_ref[...]
    vzo_ref[...] = (_dot(xb, wvzo_ref[...]) + bvzo_ref[...]).astype(BF16)
    gc_ref[...], gr_ref[...] = _gates(xb, wg_ref, wgt_ref, bg_ref, bgt_ref)


def _meta_proj(meta, g_pre, wts):
    n = meta.shape[0]
    shapes = ((n, 4 * NA_WIDTH, BF16), (n, 2 * ML_WIDTH, F32), (n, 3 * ML_WIDTH, BF16),
              (n, N_GATES, F32), (N_GATES, n, F32))
    return pl.pallas_call(
        _meta_proj_kernel,
        grid=(1,),
        in_specs=[_resident(meta), _resident(g_pre)] + [_resident(a) for a in wts],
        out_specs=tuple(pl.BlockSpec((r, c), lambda i: (0, 0)) for r, c, _ in shapes),
        out_shape=tuple(jax.ShapeDtypeStruct((r, c), d) for r, c, d in shapes),
        compiler_params=_params(1),
        name="meta_proj",
    )(meta, g_pre, *wts)


def _na_row_start(r, n_rows):
    return jnp.clip(r - NA_WIN_ROWS // 2, 0, n_rows - NA_WIN_ROWS)


def _na_kernel(q_ref, k_ref, v_ref, z_ref, mk_ref, mv_ref, bias_ref, o_ref, *, n_rows, rows_per_step):
    n_keys = NA_WIN_ROWS * GRID_W
    scale = NA_DH ** -0.5
    lane = lax.broadcasted_iota(jnp.int32, (GRID_W, NA_PAIR), 1)
    lo = lane < NA_DH
    n_pairs = NA_WIDTH // NA_PAIR
    groups = [slice(g * NA_PAIR, (g + 1) * NA_PAIR) for g in range(n_pairs)]

    def window(i):
        r = pl.program_id(1) * rows_per_step + i
        r0 = _na_row_start(r, n_rows)
        return pl.multiple_of(r0 * GRID_W, GRID_W), r0 - r + NA_WIN_ROWS - 1

    def scores(i):
        start, _ = window(i)
        qrows = slice(i * GRID_W, (i + 1) * GRID_W)
        out = []
        for cols in groups:
            qf = q_ref[qrows, cols].astype(F32) * scale
            q2 = jnp.concatenate([jnp.where(lo, qf, 0.0), jnp.where(lo, 0.0, qf)], axis=0).astype(BF16)
            s = _dot_nt(q2, k_ref[pl.ds(start, n_keys), cols])
            sm = _dot_nt(q2, mk_ref[:, cols])
            out.append((s, sm))
        return out

    def softmax(i, sc):
        _, off = window(i)
        out = []
        for g, (s, sm) in enumerate(sc):
            bias = bias_ref[off, g]
            s = jnp.where(bias > 0.5 * NEG, s + bias, NEG)
            m = jnp.maximum(jnp.max(s, axis=-1, keepdims=True), jnp.max(sm, axis=-1, keepdims=True))
            p = jnp.exp(s - m)
            pm = jnp.exp(sm - m)
            denom = jnp.sum(p, axis=-1, keepdims=True) + jnp.sum(pm, axis=-1, keepdims=True)
            out.append((p.astype(BF16), pm.astype(BF16), denom))
        return out

    def values(i, probs):
        start, _ = window(i)
        qrows = slice(i * GRID_W, (i + 1) * GRID_W)
        for cols, (p, pm, denom) in zip(groups, probs):
            o2 = _dot(p, v_ref[pl.ds(start, n_keys), cols]) + _dot(pm, mv_ref[:, cols])
            o2 = o2 / denom
            o = jnp.where(lo, o2[0:GRID_W], o2[GRID_W:2 * GRID_W])
            o_ref[qrows, cols] = (o * _silu(z_ref[qrows, cols].astype(F32))).astype(BF16)

    sc = scores(0)
    for i in range(rows_per_step):
        nxt = scores(i + 1) if i + 1 < rows_per_step else None
        values(i, softmax(i, sc))
        sc = nxt


def _na_bias_table(rpb):
    c = np.arange(GRID_W)
    c0 = np.clip(c - NA_WIN_COLS // 2, 0, GRID_W - NA_WIN_COLS)
    kc = np.arange(GRID_W)
    valid = (kc[None, :] >= c0[:, None]) & (kc[None, :] < c0[:, None] + NA_WIN_COLS)
    dc = kc[None, :] - c[:, None] + NA_WIN_COLS - 1
    n_dc = 2 * NA_WIN_COLS - 1
    n_dr = 2 * NA_WIN_ROWS - 1
    onehot = ((dc[None] == np.arange(n_dc)[:, None, None]) & valid[None]).astype(np.float32)
    toe = jnp.dot(rpb.astype(F32).reshape(NA_HEADS * n_dr, n_dc), onehot.reshape(n_dc, GRID_W * GRID_W),
                  precision=lax.Precision.HIGHEST)
    toe = jnp.where(valid.reshape(1, GRID_W * GRID_W), toe, NEG)
    toe = toe.reshape(NA_HEADS, n_dr, GRID_W, GRID_W)
    tab = jnp.stack([toe[:, off:off + NA_WIN_ROWS] for off in range(NA_WIN_ROWS)])
    tab = jnp.transpose(tab, (0, 1, 3, 2, 4))
    return tab.reshape(NA_WIN_ROWS, NA_WIDTH // NA_PAIR, 2 * GRID_W, NA_WIN_ROWS * GRID_W)


def _na(na, meta_na, bias_tab, *, batch, seq):
    n_rows = seq // GRID_W
    rps = NA_ROWS_PER_STEP
    assert n_rows >= NA_WIN_ROWS and seq % GRID_W == 0 and n_rows % rps == 0
    steps = n_rows // rps
    qz = lambda col: pl.BlockSpec((rps * GRID_W, NA_WIDTH), lambda b, r: (b * steps + r, col))
    kv = lambda col: pl.BlockSpec((seq, NA_WIDTH), lambda b, r: (b, col))
    meta = lambda col: pl.BlockSpec((N_META, NA_WIDTH), lambda b, r: (0, col), pipeline_mode=pl.Buffered(1))
    return pl.pallas_call(
        functools.partial(_na_kernel, n_rows=n_rows, rows_per_step=rps),
        grid=(batch, steps),
        in_specs=[qz(0), kv(1), kv(2), qz(3), meta(1), meta(2), _resident(bias_tab)],
        out_specs=qz(0),
        out_shape=jax.ShapeDtypeStruct((batch * seq, NA_WIDTH), BF16),
        compiler_params=_params(2),
        name="na",
    )(na, na, na, na, meta_na, meta_na, bias_tab)


def _conv_silu(x, prev8, next8, w, xs_ref):
    tb = x.shape[0]
    xs_ref[0:8, :] = prev8
    xs_ref[8:tb + 8, :] = x
    xs_ref[tb + 8:tb + 16, :] = next8
    y = (w[0:1] * xs_ref[7:tb + 7, :] + w[1:2] * xs_ref[8:tb + 8, :]
         + w[2:3] * xs_ref[9:tb + 9, :])
    return _silu(y)


def _twice(row):
    return jnp.concatenate([row, row], axis=1)


def _ml_block(q_all, k_all, v_ref, gc_ref, gr_ref, ct_ref, m_ref, emit, *, reverse, fillers=()):
    tb = q_all.shape[0]
    ch = ML_CHUNK
    n_chunks = tb // ch
    order = range(n_chunks - 1, -1, -1) if reverse else range(n_chunks)
    ig_i, b_i = (2 * ML_HEADS, 3 * ML_HEADS) if reverse else (0, ML_HEADS)
    ones = jnp.ones((ch, ML_DH), BF16)
    ri = lax.broadcasted_iota(jnp.int32, (ch, ch), 0)
    ci = lax.broadcasted_iota(jnp.int32, (ch, ch), 1)
    causal = (ci >= ri) if reverse else (ci <= ri)
    eye = ci == ri
    heads = range(ML_HEADS)
    cts = [ct_ref[hd] for hd in heads]
    ms = [m_ref[hd][0:1, :] for hd in heads]
    work = []
    for c in order:
        rows = slice(c * ch, (c + 1) * ch)
        for hd in heads:
            cols = slice(hd * ML_DH, (hd + 1) * ML_DH)
            bc = jnp.broadcast_to(gc_ref[rows, b_i + hd:b_i + hd + 1], (ch, ch))
            r = gr_ref[ig_i + hd:ig_i + hd + 1, rows] - gr_ref[b_i + hd:b_i + hd + 1, rows]
            g = bc[0:1] if reverse else bc[ch - 1:ch]
            logd = jnp.where(causal, bc + jnp.broadcast_to(r, (ch, ch)), NEG)
            rowmax = jnp.max(logd, axis=1, keepdims=True)
            a = g + r
            m = ms[hd]
            m_new = jnp.maximum(g + m, jnp.max(a, axis=1, keepdims=True))
            ms[hd] = m_new
            w_diag = jnp.where(eye, jnp.exp(a - m_new), 0.0).astype(BF16)
            work.append(dict(hd=hd, rows=rows, cols=cols, bc=bc, logd=logd, rowmax=rowmax, m=m,
                             fs=jnp.exp(g + m - m_new), w_diag=w_diag,
                             k=k_all[rows, cols].astype(BF16), q=q_all[rows, cols].astype(BF16),
                             v_ext=jnp.concatenate([v_ref[rows, cols], ones], axis=1)))
    if len(fillers) > 0:
        fillers[0]()
    for w in work:
        w["wv"] = _dot(w["w_diag"], w["v_ext"]).astype(BF16)
    for w in work:
        w["qk"] = _dot_nt(w["q"], w["k"])
    for w in work:
        w["dct"] = _dot_tn(w["k"], w["wv"])
    for w in work:
        w["r2"] = _dot(w["q"], cts[w["hd"]].astype(BF16))
        cts[w["hd"]] = _twice(w["fs"]) * cts[w["hd"]] + w["dct"]
    if len(fillers) > 1:
        fillers[1]()
    for w in work:
        inter = w["bc"] + w["m"]
        m_t = jnp.maximum(w["rowmax"], inter)
        w["e_inter"] = jnp.exp(inter - m_t)
        w["floor"] = jnp.exp(-m_t)
        s = (w["qk"] * jnp.exp(w["logd"] - m_t)).astype(BF16)
        w["r1"] = _dot(s, w["v_ext"])
    for w in work:
        num = w["r1"][:, :ML_DH] + w["e_inter"] * w["r2"][:, :ML_DH]
        den = w["r1"][:, ML_DH:] + w["e_inter"] * w["r2"][:, ML_DH:]
        emit(w["hd"], w["rows"], w["cols"], num / jnp.maximum(jnp.abs(den), w["floor"]))
    for hd in heads:
        ct_ref[hd] = cts[hd]
        m_ref[hd] = jnp.broadcast_to(ms[hd], m_ref.shape[1:])


def _proj_ml_fwd_kernel(x_ref, xnext_ref, g_ref, wna_ref, wqk_ref, wvzo_ref, wmg_ref, wg_ref, wgt_ref,
                        bna_ref, bqk_ref, bvzo_ref, bmg_ref, bg_ref, bgt_ref,
                        mqk_ref, mv_ref, mgc_ref, mgr_ref, cw_ref,
                        na_ref, vzo_ref, mg_ref, gc_ref, gr_ref, hf_ref, qc_ref, kc_ref,
                        ct_ref, m_ref, xs_ref, tail_ref, *, n_blocks):
    i = pl.program_id(1)
    tb = x_ref.shape[0]
    first = i == 0
    last = i == n_blocks - 1

    @pl.when(first)
    def _():
        w = cw_ref[:, ML_WIDTH:2 * ML_WIDTH]
        xs_ref[0:8, :] = jnp.zeros((8, ML_WIDTH), F32)
        xs_ref[8:8 + N_META, :] = mqk_ref[:, ML_WIDTH:2 * ML_WIDTH]
        kcols = slice(ML_WIDTH, 2 * ML_WIDTH)
        head = _dot(_rmsnorm_bf16(x_ref[0:16, :], g_ref[...]), wqk_ref[:, kcols]) + bqk_ref[:, kcols]
        xs_ref[8 + N_META:16 + N_META, :] = head[0:8]
        y = (w[0:1] * xs_ref[7:7 + N_META, :] + w[1:2] * xs_ref[8:8 + N_META, :]
             + w[2:3] * xs_ref[9:9 + N_META, :])
        mk = _silu(y) * (ML_DH ** -0.5)
        ones = jnp.ones((N_META, ML_DH), BF16)
        ri = lax.broadcasted_iota(jnp.int32, (N_META, N_META), 0)
        ci = lax.broadcasted_iota(jnp.int32, (N_META, N_META), 1)
        for hd in range(ML_HEADS):
            cols = slice(hd * ML_DH, (hd + 1) * ML_DH)
            ig = mgc_ref[:, hd:hd + 1]
            lf = mgr_ref[ML_HEADS + hd:ML_HEADS + hd + 1, :]
            a = jnp.sum(jnp.where(ci > ri, lf, 0.0), axis=1, keepdims=True) + ig
            g = jnp.sum(lf, axis=1, keepdims=True)
            m_loc = jnp.max(a, axis=0, keepdims=True)
            m0 = jnp.maximum(g, m_loc)
            v_ext = jnp.concatenate([mv_ref[:, cols], ones], axis=1)
            dct = _dot_tn((jnp.exp(a - m_loc) * mk[:, cols]).astype(BF16), v_ext)
            ct_ref[hd] = jnp.exp(m_loc - m0) * dct
            m_ref[hd] = jnp.broadcast_to(m0, m_ref.shape[1:])
        tail_ref[...] = mqk_ref[8:16, :]

    xb_ext = _rmsnorm_bf16(jnp.concatenate([x_ref[...], xnext_ref[...]], axis=0), g_ref[...])
    xb = xb_ext[0:tb]
    qk_ext = _dot(xb_ext, wqk_ref[...]) + bqk_ref[...]
    vzo_ref[...] = (_dot(xb, wvzo_ref[...]) + bvzo_ref[...]).astype(BF16)
    gc, gr = _gates(xb, wg_ref, wgt_ref, bg_ref, bgt_ref)
    _store_gate_cumsums(gc, gr, gc_ref, gr_ref, ML_CHUNK)
    qp, kp = qk_ext[:, 0:ML_WIDTH], qk_ext[:, ML_WIDTH:2 * ML_WIDTH]

    prev8 = tail_ref[...]
    zeros8 = jnp.zeros((8, ML_WIDTH), F32)
    qnext = jnp.where(last, zeros8, qp[tb:tb + 8])
    knext = jnp.where(last, zeros8, kp[tb:tb + 8])
    q_all = _conv_silu(qp[0:tb], prev8[:, 0:ML_WIDTH], qnext, cw_ref[:, 0:ML_WIDTH], xs_ref).astype(BF16)
    k_all = _conv_silu(kp[0:tb], prev8[:, ML_WIDTH:2 * ML_WIDTH], knext,
                       cw_ref[:, ML_WIDTH:2 * ML_WIDTH], xs_ref) * (ML_DH ** -0.5)
    tail_ref[...] = qk_ext[tb - 8:tb]
    qc_ref[...] = q_all
    kc_ref[...] = k_all.astype(BF16)

    def emit(hd, rows, cols, h):
        hf_ref[rows, cols] = h

    def project_na():
        na_ref[...] = (_dot(xb, wna_ref[...]) + bna_ref[...]).astype(BF16)

    def project_mg():
        mg_ref[...] = (_dot(xb, wmg_ref[...]) + bmg_ref[...]).astype(BF16)

    _ml_block(q_all, k_all, vzo_ref, gc_ref, gr_ref, ct_ref, m_ref, emit, reverse=False,
              fillers=(project_na, project_mg))


def _merge_project(x, ya_in, yb_in, mg_ref, wa_ref, wb_ref, wo_ref, g_ref):
    ya = _dot(ya_in, wa_ref[...])
    yb = _dot(yb_in, wb_ref[...])
    ga = _sigmoid(mg_ref[:, 0:D_MODEL].astype(F32))
    gb = _sigmoid(mg_ref[:, D_MODEL:2 * D_MODEL].astype(F32))
    y = _dot((ga * ya + gb * yb).astype(BF16), wo_ref[...])
    return x + y * lax.rsqrt(jnp.mean(y * y, axis=-1, keepdims=True) + EPS) * g_ref[...]


def _ml_bwd_out_kernel(qc_ref, kc_ref, v_ref, gc_ref, gr_ref, hf_ref, o_ref, z_ref, hg_ref,
                       x_ref, ya_ref, mg_ref, wa_ref, wb_ref, wo_ref, g_ref, y_ref,
                       ct_ref, m_ref, yb_ref, *, blocks_per_seq):
    s = pl.program_id(0)

    @pl.when(s % blocks_per_seq == 0)
    def _():
        ct_ref[...] = jnp.zeros(ct_ref.shape, F32)
        m_ref[...] = jnp.zeros(m_ref.shape, F32)

    @pl.when(s == 0)
    def _():
        yb_ref[...] = jnp.zeros(yb_ref.shape, BF16)

    y_ref[...] = _merge_project(x_ref[...], ya_ref[...], yb_ref[...], mg_ref, wa_ref, wb_ref, wo_ref, g_ref)

    def emit(hd, rows, cols, h):
        hh = _sigmoid(o_ref[rows, cols].astype(F32)) * (hf_ref[rows, cols] + h)
        mu = jnp.mean(hh, axis=-1, keepdims=True)
        d = hh - mu
        var = jnp.mean(d * d, axis=-1, keepdims=True)
        y = d * lax.rsqrt(var + EPS) * hg_ref[:, cols] * _silu(z_ref[rows, cols].astype(F32))
        yb_ref[rows, cols] = y.astype(BF16)

    _ml_block(qc_ref[...], kc_ref[...], v_ref, gc_ref, gr_ref, ct_ref, m_ref, emit, reverse=True)


def _ml_state_scratch():
    return [pltpu.VMEM((ML_HEADS, ML_DH, 2 * ML_DH), F32), pltpu.VMEM((ML_HEADS, 8, 128), F32)]


def _proj_ml_fwd(x2d, g_pre, wts, meta_qk, meta_vzo, meta_gc, meta_gr, conv_w, *, batch, seq, tb):
    nb = seq // tb
    halo = 16
    rows = batch * seq
    blk = lambda b, i: b * nb + i
    row = lambda width: pl.BlockSpec((tb, width), lambda b, i: (blk(b, i), 0))
    nxt = pl.BlockSpec((halo, D_MODEL),
                       lambda b, i: (jnp.minimum((blk(b, i) + 1) * (tb // halo), rows // halo - 1), 0))
    out_shape = (
        jax.ShapeDtypeStruct((rows, 4 * NA_WIDTH), BF16),
        jax.ShapeDtypeStruct((rows, 3 * ML_WIDTH), BF16),
        jax.ShapeDtypeStruct((rows, 2 * D_MODEL), BF16),
        jax.ShapeDtypeStruct((rows, N_GATES), F32),
        jax.ShapeDtypeStruct((N_GATES, rows), F32),
        jax.ShapeDtypeStruct((rows, ML_WIDTH), F32),
        jax.ShapeDtypeStruct((rows, ML_WIDTH), BF16),
        jax.ShapeDtypeStruct((rows, ML_WIDTH), BF16),
    )
    out_specs = (row(4 * NA_WIDTH), row(3 * ML_WIDTH), row(2 * D_MODEL), row(N_GATES),
                 pl.BlockSpec((N_GATES, tb), lambda b, i: (0, blk(b, i))),
                 row(ML_WIDTH), row(ML_WIDTH), row(ML_WIDTH))
    return pl.pallas_call(
        functools.partial(_proj_ml_fwd_kernel, n_blocks=nb),
        grid=(batch, nb),
        in_specs=[row(D_MODEL), nxt, _resident(g_pre)] + [_resident(a) for a in wts]
                 + [_resident(meta_qk), pl.BlockSpec((N_META, ML_WIDTH), lambda b, i: (0, 0)),
                    _resident(meta_gc), _resident(meta_gr), _resident(conv_w)],
        out_specs=out_specs,
        out_shape=out_shape,
        scratch_shapes=_ml_state_scratch() + [pltpu.VMEM((tb + 16, ML_WIDTH), F32),
                                              pltpu.VMEM((8, 2 * ML_WIDTH), F32)],
        compiler_params=_params(2),
        name="proj_ml_fwd",
    )(x2d, x2d, g_pre, *wts, meta_qk, meta_vzo, meta_gc, meta_gr, conv_w)


def _ml_bwd_out(qc, kc, vzo, gc, gr, hf, head_g, x2d, ya, mg, w_a, w_b, w_out, g_post, *, batch, seq, tb):
    nb = seq // tb
    n = batch * nb

    def row_block(t):
        return (t // nb) * nb + nb - 1 - t % nb

    ml_blk = lambda s: row_block(jnp.minimum(s, n - 1))
    out_blk = lambda s: row_block(jnp.maximum(s, 1) - 1)
    main = lambda col: pl.BlockSpec((tb, ML_WIDTH), lambda s: (ml_blk(s), col))
    wide = lambda width: pl.BlockSpec((tb, width), lambda s: (out_blk(s), 0))
    return pl.pallas_call(
        functools.partial(_ml_bwd_out_kernel, blocks_per_seq=nb),
        grid=(n + 1,),
        in_specs=[main(0), main(0), main(0),
                  pl.BlockSpec((tb, N_GATES), lambda s: (ml_blk(s), 0)),
                  pl.BlockSpec((N_GATES, tb), lambda s: (0, ml_blk(s))),
                  main(0), main(2), main(1), _resident(head_g),
                  wide(D_MODEL), wide(NA_WIDTH), wide(2 * D_MODEL),
                  _resident(w_a), _resident(w_b), _resident(w_out), _resident(g_post)],
        out_specs=wide(D_MODEL),
        out_shape=jax.ShapeDtypeStruct((batch * seq, D_MODEL), F32),
        scratch_shapes=_ml_state_scratch() + [pltpu.VMEM((tb, ML_WIDTH), BF16)],
        compiler_params=_params(1),
        name="ml_bwd_out",
    )(qc, kc, vzo, gc, gr, hf, vzo, vzo, head_g, x2d, ya, mg, w_a, w_b, w_out, g_post)


def _split_in_proj(w_in, b_in):
    o_qk = 4 * NA_WIDTH
    o_vzo = o_qk + 2 * ML_WIDTH
    o_g = o_vzo + 3 * ML_WIDTH
    o_mg = o_g + N_GATES
    wb = w_in.astype(BF16)
    b = b_in.astype(F32)[None, :]
    wg = wb[:, o_g:o_mg]
    bg = b[:, o_g:o_mg]
    return (wb[:, :o_qk], wb[:, o_qk:o_vzo], wb[:, o_vzo:o_g], wb[:, o_mg:], wg, wg.T,
            b[:, :o_qk], b[:, o_qk:o_vzo], b[:, o_vzo:o_g], b[:, o_mg:], bg, bg.T)


def _trunk(x, meta_proj, wts, bias_tab, g_pre, conv_w, head_g, w_a, w_b, w_out, g_post):
    batch, seq, _ = x.shape
    assert seq % FWD_BLOCK == 0 and seq % BWD_BLOCK == 0 and FWD_BLOCK % ML_CHUNK == 0 and BWD_BLOCK % ML_CHUNK == 0
    x2d = x.reshape(batch * seq, D_MODEL)
    m_na, m_qk, m_vzo, m_gc, m_gr = meta_proj
    na, vzo, mg, gc, gr, hf, qc, kc = _proj_ml_fwd(x2d, g_pre, wts, m_qk, m_vzo, m_gc, m_gr, conv_w,
                                                   batch=batch, seq=seq, tb=FWD_BLOCK)
    ya = _na(na, m_na, bias_tab, batch=batch, seq=seq)
    y = _ml_bwd_out(qc, kc, vzo, gc, gr, hf, head_g, x2d, ya, mg, w_a, w_b, w_out, g_post,
                    batch=batch, seq=seq, tb=BWD_BLOCK)
    return y.reshape(batch, seq, D_MODEL)


def kernel(x_prompt, x_sample, meta_tokens, g_pre, w_in, b_in, na_rpb, ml_conv_w, ml_head_g, w_a, w_b, w_out, g_post):
    assert w_in.shape[0] == 1, "the meta-token outputs are only droppable for a single layer"
    wts = _split_in_proj(w_in[0], b_in[0])
    gp = g_pre[0].astype(F32)[None, :]
    meta_proj = _meta_proj(meta_tokens.astype(F32), gp, wts)
    bias_tab = _na_bias_table(na_rpb[0])
    args = (meta_proj, wts, bias_tab, gp, ml_conv_w[0].astype(F32), ml_head_g[0].astype(F32)[None, :],
            w_a[0].astype(BF16), w_b[0].astype(BF16), w_out[0].astype(BF16), g_post[0].astype(F32)[None, :])
    return (_trunk(x_prompt, *args), _trunk(x_sample, *args))
```

```python
import functools

import jax
import jax.numpy as jnp
import numpy as np
from jax import lax
from jax.experimental import pallas as pl
from jax.experimental.pallas import tpu as pltpu

D_MODEL = 1024
N_META = 16
GRID_W = 64
EPS = 1e-6
NEG = -1e30
NA_HEADS = 8
NA_DH = 64
NA_WIDTH = NA_HEADS * NA_DH
NA_PAIR = 2 * NA_DH
NA_WIN_ROWS = 8
NA_WIN_COLS = 16
NA_ROWS_PER_STEP = 8
ML_HEADS = 4
ML_DH = 128
ML_WIDTH = ML_HEADS * ML_DH
N_GATES = 4 * ML_HEADS
ML_CHUNK = 128
FWD_BLOCK = 512
BWD_BLOCK = 512
VMEM_LIMIT_BYTES = 56 * 1024 * 1024

F32 = jnp.float32
BF16 = jnp.bfloat16


def _dot(a, b):
    return jnp.dot(a, b, preferred_element_type=F32)


def _dot_nt(a, b):
    return lax.dot_general(a, b, (((1,), (1,)), ((), ())), preferred_element_type=F32)


def _dot_tn(a, b):
    return lax.dot_general(a, b, (((0,), (0,)), ((), ())), preferred_element_type=F32)


def _split3(x):
    hi = x.astype(BF16)
    r1 = x - hi.astype(F32)
    mid = r1.astype(BF16)
    lo = (r1 - mid.astype(F32)).astype(BF16)
    return hi, mid, lo


def _sigmoid(x):
    return 0.5 * jnp.tanh(0.5 * x) + 0.5


def _silu(x):
    return x * _sigmoid(x)


def _log_sigmoid(x):
    return jnp.minimum(x, 0.0) - jnp.log1p(jnp.exp(-jnp.abs(x)))


def _resident(a):
    return pl.BlockSpec(a.shape, lambda *_: (0,) * a.ndim, pipeline_mode=pl.Buffered(1))


def _log2(n):
    assert n & (n - 1) == 0
    return n.bit_length() - 1


def _params(n_grid_axes):
    return pltpu.CompilerParams(dimension_semantics=("arbitrary",) * n_grid_axes,
                                vmem_limit_bytes=VMEM_LIMIT_BYTES)


def _rmsnorm_bf16(x, g):
    return (x * lax.rsqrt(jnp.mean(x * x, axis=-1, keepdims=True) + EPS) * g).astype(BF16)


def _gate_rows(xb, wgt_ref, bgt_ref):
    gr = _dot_nt(wgt_ref[...], xb) + bgt_ref[...]
    kind = lax.broadcasted_iota(jnp.int32, gr.shape, 0) >> _log2(ML_HEADS)
    return jnp.where((kind & 1) == 1, _log_sigmoid(gr), gr)


def _gate_cols(xb, wg_ref, bg_ref):
    gc = _dot(xb, wg_ref[...]) + bg_ref[...]
    kind = lax.broadcasted_iota(jnp.int32, gc.shape, 1) >> _log2(ML_HEADS)
    return jnp.where((kind & 1) == 1, _log_sigmoid(gc), gc)


def _store_gate_cumsums(gr, gc_ref, gr_ref, chunk):
    ri = lax.broadcasted_iota(jnp.int32, (chunk, chunk), 0)
    ci = lax.broadcasted_iota(jnp.int32, (chunk, chunk), 1)
    lower = jnp.where(ci <= ri, 1.0, 0.0).astype(BF16)
    upper = jnp.where(ci >= ri, 1.0, 0.0).astype(BF16)
    kr = lax.broadcasted_iota(jnp.int32, (N_GATES, chunk), 0) >> _log2(ML_HEADS)
    pad = jnp.zeros((chunk - N_GATES, chunk), F32)
    for c in range(gr.shape[1] // chunk):
        rows = slice(c * chunk, (c + 1) * chunk)
        pieces = _split3(gr[:, rows])
        pre_r = sum(_dot(piece, upper) for piece in pieces)
        suf_r = sum(_dot(piece, lower) for piece in pieces)
        out = jnp.where(kr == 1, pre_r, jnp.where(kr == 3, suf_r, gr[:, rows]))
        gr_ref[:, rows] = out
        gc_ref[rows, :] = jnp.concatenate([out, pad], axis=0).T[:, 0:N_GATES]


def _meta_proj_kernel(x_ref, g_ref, wna_ref, wqk_ref, wvzo_ref, wmg_ref, wg_ref, wgt_ref,
                      bna_ref, bqk_ref, bvzo_ref, bmg_ref, bg_ref, bgt_ref,
                      na_ref, qk_ref, vzo_ref, gc_ref, gr_ref):
    xb = _rmsnorm_bf16(x_ref[...], g_ref[...])
    na_ref[...] = (_dot(xb, wna_ref[...]) + bna_ref[...]).astype(BF16)
    qk_ref[...] = _dot(xb, wqk_ref[...]) + bqk_ref[...]
    vzo_ref[...] = (_dot(xb, wvzo_ref[...]) + bvzo_ref[...]).astype(BF16)
    gc_ref[...] = _gate_cols(xb, wg_ref, bg_ref)
    gr_ref[...] = _gate_rows(xb, wgt_ref, bgt_ref)


def _meta_proj(meta, g_pre, wts):
    n = meta.shape[0]
    shapes = ((n, 4 * NA_WIDTH, BF16), (n, 2 * ML_WIDTH, F32), (n, 3 * ML_WIDTH, BF16),
              (n, N_GATES, F32), (N_GATES, n, F32))
    return pl.pallas_call(
        _meta_proj_kernel,
        grid=(1,),
        in_specs=[_resident(meta), _resident(g_pre)] + [_resident(a) for a in wts],
        out_specs=tuple(pl.BlockSpec((r, c), lambda i: (0, 0)) for r, c, _ in shapes),
        out_shape=tuple(jax.ShapeDtypeStruct((r, c), d) for r, c, d in shapes),
        compiler_params=_params(1),
        name="meta_proj",
    )(meta, g_pre, *wts)


def _na_row_start(r, n_rows):
    return jnp.clip(r - NA_WIN_ROWS // 2, 0, n_rows - NA_WIN_ROWS)


def _na_kernel(q_ref, k_ref, v_ref, z_ref, mk_ref, mv_ref, bias_ref, o_ref, *, n_rows, rows_per_step):
    n_keys = NA_WIN_ROWS * GRID_W
    scale = NA_DH ** -0.5
    lane = lax.broadcasted_iota(jnp.int32, (GRID_W, NA_PAIR), 1)
    lo = lane < NA_DH
    n_pairs = NA_WIDTH // NA_PAIR
    groups = [slice(g * NA_PAIR, (g + 1) * NA_PAIR) for g in range(n_pairs)]

    def window(i):
        r = pl.program_id(1) * rows_per_step + i
        r0 = _na_row_start(r, n_rows)
        return pl.multiple_of(r0 * GRID_W, GRID_W), r0 - r + NA_WIN_ROWS - 1

    def scores(i):
        start, _ = window(i)
        qrows = slice(i * GRID_W, (i + 1) * GRID_W)
        out = []
        for cols in groups:
            qf = q_ref[qrows, cols].astype(F32) * scale
            q2 = jnp.concatenate([jnp.where(lo, qf, 0.0), jnp.where(lo, 0.0, qf)], axis=0).astype(BF16)
            s = _dot_nt(q2, k_ref[pl.ds(start, n_keys), cols])
            sm = _dot_nt(q2, mk_ref[:, cols])
            out.append((s, sm))
        return out

    def softmax(i, sc):
        _, off = window(i)
        out = []
        for g, (s, sm) in enumerate(sc):
            bias = bias_ref[off, g]
            s = jnp.where(bias > 0.5 * NEG, s + bias, NEG)
            m = jnp.maximum(jnp.max(s, axis=-1, keepdims=True), jnp.max(sm, axis=-1, keepdims=True))
            p = jnp.exp(s - m)
            pm = jnp.exp(sm - m)
            denom = jnp.sum(p, axis=-1, keepdims=True) + jnp.sum(pm, axis=-1, keepdims=True)
            out.append((p.astype(BF16), pm.astype(BF16), denom))
        return out

    def values(i, probs):
        start, _ = window(i)
        qrows = slice(i * GRID_W, (i + 1) * GRID_W)
        for cols, (p, pm, denom) in zip(groups, probs):
            o2 = _dot(p, v_ref[pl.ds(start, n_keys), cols]) + _dot(pm, mv_ref[:, cols])
            o2 = o2 / denom
            o = jnp.where(lo, o2[0:GRID_W], o2[GRID_W:2 * GRID_W])
            o_ref[qrows, cols] = (o * _silu(z_ref[qrows, cols].astype(F32))).astype(BF16)

    sc = scores(0)
    for i in range(rows_per_step):
        nxt = scores(i + 1) if i + 1 < rows_per_step else None
        values(i, softmax(i, sc))
        sc = nxt


def _na_bias_table(rpb):
    c = np.arange(GRID_W)
    c0 = np.clip(c - NA_WIN_COLS // 2, 0, GRID_W - NA_WIN_COLS)
    kc = np.arange(GRID_W)
    valid = (kc[None, :] >= c0[:, None]) & (kc[None, :] < c0[:, None] + NA_WIN_COLS)
    dc = kc[None, :] - c[:, None] + NA_WIN_COLS - 1
    n_dc = 2 * NA_WIN_COLS - 1
    n_dr = 2 * NA_WIN_ROWS - 1
    onehot = ((dc[None] == np.arange(n_dc)[:, None, None]) & valid[None]).astype(np.float32)
    toe = jnp.einsum("hrd,dck->hcrk", rpb.astype(F32), onehot, precision=lax.Precision.HIGHEST)
    toe = jnp.where(valid[None, :, None, :], toe, NEG).reshape(NA_HEADS, GRID_W, n_dr * GRID_W)
    n_keys = NA_WIN_ROWS * GRID_W
    tab = jnp.stack([toe[:, :, off * GRID_W:off * GRID_W + n_keys] for off in range(NA_WIN_ROWS)])
    return tab.reshape(NA_WIN_ROWS, NA_WIDTH // NA_PAIR, 2 * GRID_W, n_keys)


def _na(na, meta_na, bias_tab, *, batch, seq):
    n_rows = seq // GRID_W
    rps = NA_ROWS_PER_STEP
    assert n_rows >= NA_WIN_ROWS and seq % GRID_W == 0 and n_rows % rps == 0
    steps = n_rows // rps
    qz = lambda col: pl.BlockSpec((rps * GRID_W, NA_WIDTH), lambda b, r: (b * steps + r, col))
    kv = lambda col: pl.BlockSpec((seq, NA_WIDTH), lambda b, r: (b, col))
    meta = lambda col: pl.BlockSpec((N_META, NA_WIDTH), lambda b, r: (0, col), pipeline_mode=pl.Buffered(1))
    return pl.pallas_call(
        functools.partial(_na_kernel, n_rows=n_rows, rows_per_step=rps),
        grid=(batch, steps),
        in_specs=[qz(0), kv(1), kv(2), qz(3), meta(1), meta(2), _resident(bias_tab)],
        out_specs=qz(0),
        out_shape=jax.ShapeDtypeStruct((batch * seq, NA_WIDTH), BF16),
        compiler_params=_params(2),
        name="na",
    )(na, na, na, na, meta_na, meta_na, bias_tab)


def _conv_silu(x, prev8, next8, w, xs_ref):
    tb = x.shape[0]
    xs_ref[0:8, :] = prev8
    xs_ref[8:tb + 8, :] = x
    xs_ref[tb + 8:tb + 16, :] = next8
    y = (w[0:1] * xs_ref[7:tb + 7, :] + w[1:2] * xs_ref[8:tb + 8, :]
         + w[2:3] * xs_ref[9:tb + 9, :])
    return _silu(y)


def _twice(row):
    return jnp.concatenate([row, row], axis=1)


def _ml_block(q_all, k_all, v_ref, gc_ref, gr_ref, ct_ref, m_ref, emit, *, reverse, fillers=()):
    tb = q_all.shape[0]
    ch = ML_CHUNK
    n_chunks = tb // ch
    order = range(n_chunks - 1, -1, -1) if reverse else range(n_chunks)
    ig_i, b_i = (2 * ML_HEADS, 3 * ML_HEADS) if reverse else (0, ML_HEADS)
    ones = jnp.ones((ch, ML_DH), BF16)
    ri = lax.broadcasted_iota(jnp.int32, (ch, ch), 0)
    ci = lax.broadcasted_iota(jnp.int32, (ch, ch), 1)
    causal = (ci >= ri) if reverse else (ci <= ri)
    eye = ci == ri
    heads = range(ML_HEADS)
    cts = [ct_ref[hd] for hd in heads]
    ms = [m_ref[hd][0:1, :] for hd in heads]
    work = []
    for c in order:
        rows = slice(c * ch, (c + 1) * ch)
        for hd in heads:
            cols = slice(hd * ML_DH, (hd + 1) * ML_DH)
            bc = jnp.broadcast_to(gc_ref[rows, b_i + hd:b_i + hd + 1], (ch, ch))
            r = gr_ref[ig_i + hd:ig_i + hd + 1, rows] - gr_ref[b_i + hd:b_i + hd + 1, rows]
            g = bc[0:1] if reverse else bc[ch - 1:ch]
            logd = jnp.where(causal, bc + jnp.broadcast_to(r, (ch, ch)), NEG)
            rowmax = jnp.max(logd, axis=1, keepdims=True)
            a = g + r
            m = ms[hd]
            m_new = jnp.maximum(g + m, jnp.max(a, axis=1, keepdims=True))
            ms[hd] = m_new
            w_diag = jnp.where(eye, jnp.exp(a - m_new), 0.0).astype(BF16)
            work.append(dict(hd=hd, rows=rows, cols=cols, bc=bc, logd=logd, rowmax=rowmax, m=m,
                             fs=jnp.exp(g + m - m_new), w_diag=w_diag,
                             k=k_all[rows, cols].astype(BF16), q=q_all[rows, cols].astype(BF16),
                             v_ext=jnp.concatenate([v_ref[rows, cols], ones], axis=1)))
    if len(fillers) > 0:
        fillers[0]()
    for w in work:
        w["wv"] = _dot(w["w_diag"], w["v_ext"]).astype(BF16)
    for w in work:
        w["qk"] = _dot_nt(w["q"], w["k"])
    for w in work:
        w["dct"] = _dot_tn(w["k"], w["wv"])
    for w in work:
        w["r2"] = _dot(w["q"], cts[w["hd"]].astype(BF16))
        cts[w["hd"]] = _twice(w["fs"]) * cts[w["hd"]] + w["dct"]
    if len(fillers) > 1:
        fillers[1]()
    for w in work:
        inter = w["bc"] + w["m"]
        m_t = jnp.maximum(w["rowmax"], inter)
        w["e_inter"] = jnp.exp(inter - m_t)
        w["floor"] = jnp.exp(-m_t)
        s = (w["qk"] * jnp.exp(w["logd"] - m_t)).astype(BF16)
        w["r1"] = _dot(s, w["v_ext"])
    for w in work:
        num = w["r1"][:, :ML_DH] + w["e_inter"] * w["r2"][:, :ML_DH]
        den = w["r1"][:, ML_DH:] + w["e_inter"] * w["r2"][:, ML_DH:]
        emit(w["hd"], w["rows"], w["cols"], num / jnp.maximum(jnp.abs(den), w["floor"]))
    for hd in heads:
        ct_ref[hd] = cts[hd]
        m_ref[hd] = jnp.broadcast_to(ms[hd], m_ref.shape[1:])


def _proj_ml_fwd_kernel(x_ref, xnext_ref, g_ref, wna_ref, wqk_ref, wvzo_ref, wmg_ref, wg_ref, wgt_ref,
                        bna_ref, bqk_ref, bvzo_ref, bmg_ref, bg_ref, bgt_ref,
                        mqk_ref, mv_ref, mgc_ref, mgr_ref, cw_ref,
                        na_ref, vzo_ref, mg_ref, gc_ref, gr_ref, hf_ref, qc_ref, kc_ref,
                        ct_ref, m_ref, xs_ref, tail_ref, *, n_blocks):
    i = pl.program_id(1)
    tb = x_ref.shape[0]
    first = i == 0
    last = i == n_blocks - 1

    @pl.when(first)
    def _():
        w = cw_ref[:, ML_WIDTH:2 * ML_WIDTH]
        xs_ref[0:8, :] = jnp.zeros((8, ML_WIDTH), F32)
        xs_ref[8:8 + N_META, :] = mqk_ref[:, ML_WIDTH:2 * ML_WIDTH]
        kcols = slice(ML_WIDTH, 2 * ML_WIDTH)
        head = _dot(_rmsnorm_bf16(x_ref[0:16, :], g_ref[...]), wqk_ref[:, kcols]) + bqk_ref[:, kcols]
        xs_ref[8 + N_META:16 + N_META, :] = head[0:8]
        y = (w[0:1] * xs_ref[7:7 + N_META, :] + w[1:2] * xs_ref[8:8 + N_META, :]
             + w[2:3] * xs_ref[9:9 + N_META, :])
        mk = _silu(y) * (ML_DH ** -0.5)
        ones = jnp.ones((N_META, ML_DH), BF16)
        ri = lax.broadcasted_iota(jnp.int32, (N_META, N_META), 0)
        ci = lax.broadcasted_iota(jnp.int32, (N_META, N_META), 1)
        for hd in range(ML_HEADS):
            cols = slice(hd * ML_DH, (hd + 1) * ML_DH)
            ig = mgc_ref[:, hd:hd + 1]
            lf = mgr_ref[ML_HEADS + hd:ML_HEADS + hd + 1, :]
            a = jnp.sum(jnp.where(ci > ri, lf, 0.0), axis=1, keepdims=True) + ig
            g = jnp.sum(lf, axis=1, keepdims=True)
            m_loc = jnp.max(a, axis=0, keepdims=True)
            m0 = jnp.maximum(g, m_loc)
            v_ext = jnp.concatenate([mv_ref[:, cols], ones], axis=1)
            dct = _dot_tn((jnp.exp(a - m_loc) * mk[:, cols]).astype(BF16), v_ext)
            ct_ref[hd] = jnp.exp(m_loc - m0) * dct
            m_ref[hd] = jnp.broadcast_to(m0, m_ref.shape[1:])
        tail_ref[...] = mqk_ref[8:16, :]

    xb_ext = _rmsnorm_bf16(jnp.concatenate([x_ref[...], xnext_ref[...]], axis=0), g_ref[...])
    xb = xb_ext[0:tb]
    qk_ext = _dot(xb_ext, wqk_ref[...]) + bqk_ref[...]
    vzo_ref[...] = (_dot(xb, wvzo_ref[...]) + bvzo_ref[...]).astype(BF16)
    _store_gate_cumsums(_gate_rows(xb, wgt_ref, bgt_ref), gc_ref, gr_ref, ML_CHUNK)
    qp, kp = qk_ext[:, 0:ML_WIDTH], qk_ext[:, ML_WIDTH:2 * ML_WIDTH]

    prev8 = tail_ref[...]
    zeros8 = jnp.zeros((8, ML_WIDTH), F32)
    qnext = jnp.where(last, zeros8, qp[tb:tb + 8])
    knext = jnp.where(last, zeros8, kp[tb:tb + 8])
    q_all = _conv_silu(qp[0:tb], prev8[:, 0:ML_WIDTH], qnext, cw_ref[:, 0:ML_WIDTH], xs_ref).astype(BF16)
    k_all = _conv_silu(kp[0:tb], prev8[:, ML_WIDTH:2 * ML_WIDTH], knext,
                       cw_ref[:, ML_WIDTH:2 * ML_WIDTH], xs_ref) * (ML_DH ** -0.5)
    tail_ref[...] = qk_ext[tb - 8:tb]
    qc_ref[...] = q_all
    kc_ref[...] = k_all.astype(BF16)

    def emit(hd, rows, cols, h):
        hf_ref[rows, cols] = h

    def project_na():
        na_ref[...] = (_dot(xb, wna_ref[...]) + bna_ref[...]).astype(BF16)

    def project_mg():
        mg_ref[...] = (_dot(xb, wmg_ref[...]) + bmg_ref[...]).astype(BF16)

    _ml_block(q_all, k_all, vzo_ref, gc_ref, gr_ref, ct_ref, m_ref, emit, reverse=False,
              fillers=(project_na, project_mg))


def _merge_project(x, ya_in, yb_in, mg_ref, wa_ref, wb_ref, wo_ref, g_ref):
    ya = _dot(ya_in, wa_ref[...])
    yb = _dot(yb_in, wb_ref[...])
    ga = _sigmoid(mg_ref[:, 0:D_MODEL].astype(F32))
    gb = _sigmoid(mg_ref[:, D_MODEL:2 * D_MODEL].astype(F32))
    y = _dot((ga * ya + gb * yb).astype(BF16), wo_ref[...])
    return x + y * lax.rsqrt(jnp.mean(y * y, axis=-1, keepdims=True) + EPS) * g_ref[...]


def _ml_bwd_out_kernel(qc_ref, kc_ref, v_ref, gc_ref, gr_ref, hf_ref, o_ref, z_ref, hg_ref,
                       x_ref, ya_ref, mg_ref, wa_ref, wb_ref, wo_ref, g_ref, y_ref,
                       ct_ref, m_ref, yb_ref, *, blocks_per_seq):
    s = pl.program_id(0)

    @pl.when(s % blocks_per_seq == 0)
    def _():
        ct_ref[...] = jnp.zeros(ct_ref.shape, F32)
        m_ref[...] = jnp.zeros(m_ref.shape, F32)

    @pl.when(s == 0)
    def _():
        yb_ref[...] = jnp.zeros(yb_ref.shape, BF16)

    y_ref[...] = _merge_project(x_ref[...], ya_ref[...], yb_ref[...], mg_ref, wa_ref, wb_ref, wo_ref, g_ref)

    def emit(hd, rows, cols, h):
        hh = _sigmoid(o_ref[rows, cols].astype(F32)) * (hf_ref[rows, cols] + h)
        mu = jnp.mean(hh, axis=-1, keepdims=True)
        d = hh - mu
        var = jnp.mean(d * d, axis=-1, keepdims=True)
        y = d * lax.rsqrt(var + EPS) * hg_ref[:, cols] * _silu(z_ref[rows, cols].astype(F32))
        yb_ref[rows, cols] = y.astype(BF16)

    _ml_block(qc_ref[...], kc_ref[...], v_ref, gc_ref, gr_ref, ct_ref, m_ref, emit, reverse=True)


def _ml_state_scratch():
    return [pltpu.VMEM((ML_HEADS, ML_DH, 2 * ML_DH), F32), pltpu.VMEM((ML_HEADS, 8, 128), F32)]


def _proj_ml_fwd(x2d, g_pre, wts, meta_qk, meta_vzo, meta_gc, meta_gr, conv_w, *, batch, seq, tb):
    nb = seq // tb
    halo = 16
    rows = batch * seq
    blk = lambda b, i: b * nb + i
    row = lambda width: pl.BlockSpec((tb, width), lambda b, i: (blk(b, i), 0))
    nxt = pl.BlockSpec((halo, D_MODEL),
                       lambda b, i: (jnp.minimum((blk(b, i) + 1) * (tb // halo), rows // halo - 1), 0))
    out_shape = (
        jax.ShapeDtypeStruct((rows, 4 * NA_WIDTH), BF16),
        jax.ShapeDtypeStruct((rows, 3 * ML_WIDTH), BF16),
        jax.ShapeDtypeStruct((rows, 2 * D_MODEL), BF16),
        jax.ShapeDtypeStruct((rows, N_GATES), F32),
        jax.ShapeDtypeStruct((N_GATES, rows), F32),
        jax.ShapeDtypeStruct((rows, ML_WIDTH), F32),
        jax.ShapeDtypeStruct((rows, ML_WIDTH), BF16),
        jax.ShapeDtypeStruct((rows, ML_WIDTH), BF16),
    )
    out_specs = (row(4 * NA_WIDTH), row(3 * ML_WIDTH), row(2 * D_MODEL), row(N_GATES),
                 pl.BlockSpec((N_GATES, tb), lambda b, i: (0, blk(b, i))),
                 row(ML_WIDTH), row(ML_WIDTH), row(ML_WIDTH))
    return pl.pallas_call(
        functools.partial(_proj_ml_fwd_kernel, n_blocks=nb),
        grid=(batch, nb),
        in_specs=[row(D_MODEL), nxt, _resident(g_pre)] + [_resident(a) for a in wts]
                 + [_resident(meta_qk), pl.BlockSpec((N_META, ML_WIDTH), lambda b, i: (0, 0)),
                    _resident(meta_gc), _resident(meta_gr), _resident(conv_w)],
        out_specs=out_specs,
        out_shape=out_shape,
        scratch_shapes=_ml_state_scratch() + [pltpu.VMEM((tb + 16, ML_WIDTH), F32),
                                              pltpu.VMEM((8, 2 * ML_WIDTH), F32)],
        compiler_params=_params(2),
        name="proj_ml_fwd",
    )(x2d, x2d, g_pre, *wts, meta_qk, meta_vzo, meta_gc, meta_gr, conv_w)


def _ml_bwd_out(qc, kc, vzo, gc, gr, hf, head_g, x2d, ya, mg, w_a, w_b, w_out, g_post, *, batch, seq, tb):
    nb = seq // tb
    n = batch * nb

    def row_block(t):
        return (t // nb) * nb + nb - 1 - t % nb

    ml_blk = lambda s: row_block(jnp.minimum(s, n - 1))
    out_blk = lambda s: row_block(jnp.maximum(s, 1) - 1)
    main = lambda col: pl.BlockSpec((tb, ML_WIDTH), lambda s: (ml_blk(s), col))
    wide = lambda width: pl.BlockSpec((tb, width), lambda s: (out_blk(s), 0))
    return pl.pallas_call(
        functools.partial(_ml_bwd_out_kernel, blocks_per_seq=nb),
        grid=(n + 1,),
        in_specs=[main(0), main(0), main(0),
                  pl.BlockSpec((tb, N_GATES), lambda s: (ml_blk(s), 0)),
                  pl.BlockSpec((N_GATES, tb), lambda s: (0, ml_blk(s))),
                  main(0), main(2), main(1), _resident(head_g),
                  wide(D_MODEL), wide(NA_WIDTH), wide(2 * D_MODEL),
                  _resident(w_a), _resident(w_b), _resident(w_out), _resident(g_post)],
        out_specs=wide(D_MODEL),
        out_shape=jax.ShapeDtypeStruct((batch * seq, D_MODEL), F32),
        scratch_shapes=_ml_state_scratch() + [pltpu.VMEM((tb, ML_WIDTH), BF16)],
        compiler_params=_params(1),
        name="ml_bwd_out",
    )(qc, kc, vzo, gc, gr, hf, vzo, vzo, head_g, x2d, ya, mg, w_a, w_b, w_out, g_post)


def _split_in_proj(w_in, b_in):
    o_qk = 4 * NA_WIDTH
    o_vzo = o_qk + 2 * ML_WIDTH
    o_g = o_vzo + 3 * ML_WIDTH
    o_mg = o_g + N_GATES
    b = b_in.astype(F32)[None, :]
    cut = lambda lo, hi: w_in[:, lo:hi].astype(BF16)
    wg = cut(o_g, o_mg)
    bg = b[:, o_g:o_mg]
    return (cut(0, o_qk), cut(o_qk, o_vzo), cut(o_vzo, o_g), cut(o_mg, w_in.shape[1]), wg, wg.T,
            b[:, :o_qk], b[:, o_qk:o_vzo], b[:, o_vzo:o_g], b[:, o_mg:], bg, bg.T)


def _trunk(x, meta_proj, wts, bias_tab, g_pre, conv_w, head_g, w_a, w_b, w_out, g_post):
    batch, seq, _ = x.shape
    assert seq % FWD_BLOCK == 0 and seq % BWD_BLOCK == 0 and FWD_BLOCK % ML_CHUNK == 0 and BWD_BLOCK % ML_CHUNK == 0
    x2d = x.reshape(batch * seq, D_MODEL)
    m_na, m_qk, m_vzo, m_gc, m_gr = meta_proj
    na, vzo, mg, gc, gr, hf, qc, kc = _proj_ml_fwd(x2d, g_pre, wts, m_qk, m_vzo, m_gc, m_gr, conv_w,
                                                   batch=batch, seq=seq, tb=FWD_BLOCK)
    ya = _na(na, m_na, bias_tab, batch=batch, seq=seq)
    y = _ml_bwd_out(qc, kc, vzo, gc, gr, hf, head_g, x2d, ya, mg, w_a, w_b, w_out, g_post,
                    batch=batch, seq=seq, tb=BWD_BLOCK)
    return y.reshape(batch, seq, D_MODEL)


def kernel(x_prompt, x_sample, meta_tokens, g_pre, w_in, b_in, na_rpb, ml_conv_w, ml_head_g, w_a, w_b, w_out, g_post):
    assert w_in.shape[0] == 1, "the meta-token outputs are only droppable for a single layer"
    wts = _split_in_proj(w_in[0], b_in[0])
    gp = g_pre[0].astype(F32)[None, :]
    meta_proj = _meta_proj(meta_tokens.astype(F32), gp, wts)
    bias_tab = _na_bias_table(na_rpb[0])
    args = (meta_proj, wts, bias_tab, gp, ml_conv_w[0].astype(F32), ml_head_g[0].astype(F32)[None, :],
            w_a[0].astype(BF16), w_b[0].astype(BF16), w_out[0].astype(BF16), g_post[0].astype(F32)[None, :])
    return (_trunk(x_prompt, *args), _trunk(x_sample, *args))
```

```python
import functools

import jax
import jax.numpy as jnp
import numpy as np
from jax import lax
from jax.experimental import pallas as pl
from jax.experimental.pallas import tpu as pltpu

D_MODEL = 1024
N_META = 16
GRID_W = 64
EPS = 1e-6
NEG = -1e30
NA_HEADS = 8
NA_DH = 64
NA_WIDTH = NA_HEADS * NA_DH
NA_PAIR = 2 * NA_DH
NA_WIN_ROWS = 8
NA_WIN_COLS = 16
NA_ROWS_PER_STEP = 16
ML_HEADS = 4
ML_DH = 128
ML_WIDTH = ML_HEADS * ML_DH
N_GATES = 4 * ML_HEADS
ML_CHUNK = 128
FWD_BLOCK = 512
BWD_BLOCK = 512
VMEM_LIMIT_BYTES = 58 * 1024 * 1024

F32 = jnp.float32
BF16 = jnp.bfloat16


def _dot(a, b):
    return jnp.dot(a, b, preferred_element_type=F32)


def _dot_nt(a, b):
    return lax.dot_general(a, b, (((1,), (1,)), ((), ())), preferred_element_type=F32)


def _dot_tn(a, b):
    return lax.dot_general(a, b, (((0,), (0,)), ((), ())), preferred_element_type=F32)


def _split3(x):
    hi = x.astype(BF16)
    r1 = x - hi.astype(F32)
    mid = r1.astype(BF16)
    lo = (r1 - mid.astype(F32)).astype(BF16)
    return hi, mid, lo


def _sigmoid(x):
    return 0.5 * jnp.tanh(0.5 * x) + 0.5


def _silu(x):
    return x * _sigmoid(x)


def _log_sigmoid(x):
    return jnp.minimum(x, 0.0) - jnp.log1p(jnp.exp(-jnp.abs(x)))


def _resident(a):
    return pl.BlockSpec(a.shape, lambda *_: (0,) * a.ndim, pipeline_mode=pl.Buffered(1))


def _log2(n):
    assert n & (n - 1) == 0
    return n.bit_length() - 1


def _params(n_grid_axes):
    return pltpu.CompilerParams(dimension_semantics=("arbitrary",) * n_grid_axes,
                                vmem_limit_bytes=VMEM_LIMIT_BYTES)


def _rmsnorm_bf16(x, g):
    return (x * lax.rsqrt(jnp.mean(x * x, axis=-1, keepdims=True) + EPS) * g).astype(BF16)


def _gate_rows(xb, wgt_ref, bgt_ref):
    gr = _dot_nt(wgt_ref[...], xb) + bgt_ref[...]
    kind = lax.broadcasted_iota(jnp.int32, gr.shape, 0) >> _log2(ML_HEADS)
    return jnp.where((kind & 1) == 1, _log_sigmoid(gr), gr)


def _gate_cols(xb, wg_ref, bg_ref):
    gc = _dot(xb, wg_ref[...]) + bg_ref[...]
    kind = lax.broadcasted_iota(jnp.int32, gc.shape, 1) >> _log2(ML_HEADS)
    return jnp.where((kind & 1) == 1, _log_sigmoid(gc), gc)


def _store_gate_cumsums(gr, gc_ref, gr_ref, chunk):
    ri = lax.broadcasted_iota(jnp.int32, (chunk, chunk), 0)
    ci = lax.broadcasted_iota(jnp.int32, (chunk, chunk), 1)
    lower = jnp.where(ci <= ri, 1.0, 0.0).astype(BF16)
    upper = jnp.where(ci >= ri, 1.0, 0.0).astype(BF16)
    kr = lax.broadcasted_iota(jnp.int32, (N_GATES, chunk), 0) >> _log2(ML_HEADS)
    pad = jnp.zeros((chunk - N_GATES, chunk), F32)
    for c in range(gr.shape[1] // chunk):
        rows = slice(c * chunk, (c + 1) * chunk)
        pieces = _split3(gr[:, rows])
        pre_r = sum(_dot(piece, upper) for piece in pieces)
        suf_r = sum(_dot(piece, lower) for piece in pieces)
        out = jnp.where(kr == 1, pre_r, jnp.where(kr == 3, suf_r, gr[:, rows]))
        gr_ref[:, rows] = out
        gc_ref[rows, :] = jnp.concatenate([out, pad], axis=0).T[:, 0:N_GATES]


def _meta_proj_kernel(x_ref, g_ref, wna_ref, wqk_ref, wvzo_ref, wmg_ref, wg_ref, wgt_ref,
                      bna_ref, bqk_ref, bvzo_ref, bmg_ref, bg_ref, bgt_ref,
                      na_ref, qk_ref, vzo_ref, gc_ref, gr_ref):
    xb = _rmsnorm_bf16(x_ref[...], g_ref[...])
    na_ref[...] = (_dot(xb, wna_ref[...]) + bna_ref[...]).astype(BF16)
    qk_ref[...] = _dot(xb, wqk_ref[...]) + bqk_ref[...]
    vzo_ref[...] = (_dot(xb, wvzo_ref[...]) + bvzo_ref[...]).astype(BF16)
    gc_ref[...] = _gate_cols(xb, wg_ref, bg_ref)
    gr_ref[...] = _gate_rows(xb, wgt_ref, bgt_ref)


def _meta_proj(meta, g_pre, wts):
    n = meta.shape[0]
    shapes = ((n, 4 * NA_WIDTH, BF16), (n, 2 * ML_WIDTH, F32), (n, 3 * ML_WIDTH, BF16),
              (n, N_GATES, F32), (N_GATES, n, F32))
    return pl.pallas_call(
        _meta_proj_kernel,
        grid=(1,),
        in_specs=[_resident(meta), _resident(g_pre)] + [_resident(a) for a in wts],
        out_specs=tuple(pl.BlockSpec((r, c), lambda i: (0, 0)) for r, c, _ in shapes),
        out_shape=tuple(jax.ShapeDtypeStruct((r, c), d) for r, c, d in shapes),
        compiler_params=_params(1),
        name="meta_proj",
    )(meta, g_pre, *wts)


def _na_row_start(r, n_rows):
    return jnp.clip(r - NA_WIN_ROWS // 2, 0, n_rows - NA_WIN_ROWS)


def _na_kernel(q_ref, k_ref, v_ref, z_ref, mk_ref, mv_ref, bias_ref, o_ref, *, n_rows, rows_per_step):
    n_keys = NA_WIN_ROWS * GRID_W
    scale = NA_DH ** -0.5
    lane = lax.broadcasted_iota(jnp.int32, (GRID_W, NA_PAIR), 1)
    lo = lane < NA_DH
    n_pairs = NA_WIDTH // NA_PAIR
    groups = [slice(g * NA_PAIR, (g + 1) * NA_PAIR) for g in range(n_pairs)]

    def window(i):
        r = pl.program_id(1) * rows_per_step + i
        r0 = _na_row_start(r, n_rows)
        return pl.multiple_of(r0 * GRID_W, GRID_W), r0 - r + NA_WIN_ROWS - 1

    def scores(i):
        start, _ = window(i)
        qrows = slice(i * GRID_W, (i + 1) * GRID_W)
        out = []
        for cols in groups:
            qf = q_ref[qrows, cols].astype(F32) * scale
            q2 = jnp.concatenate([jnp.where(lo, qf, 0.0), jnp.where(lo, 0.0, qf)], axis=0).astype(BF16)
            s = _dot_nt(q2, k_ref[pl.ds(start, n_keys), cols])
            sm = _dot_nt(q2, mk_ref[:, cols])
            out.append((s, sm))
        return out

    def softmax(i, sc):
        _, off = window(i)
        out = []
        for g, (s, sm) in enumerate(sc):
            bias = bias_ref[off, g]
            s = jnp.where(bias > 0.5 * NEG, s + bias, NEG)
            m = jnp.maximum(jnp.max(s, axis=-1, keepdims=True), jnp.max(sm, axis=-1, keepdims=True))
            p = jnp.exp(s - m)
            pm = jnp.exp(sm - m)
            denom = jnp.sum(p, axis=-1, keepdims=True) + jnp.sum(pm, axis=-1, keepdims=True)
            out.append((p.astype(BF16), pm.astype(BF16), denom))
        return out

    def values(i, probs):
        start, _ = window(i)
        qrows = slice(i * GRID_W, (i + 1) * GRID_W)
        for cols, (p, pm, denom) in zip(groups, probs):
            o2 = _dot(p, v_ref[pl.ds(start, n_keys), cols]) + _dot(pm, mv_ref[:, cols])
            o2 = o2 / denom
            o = jnp.where(lo, o2[0:GRID_W], o2[GRID_W:2 * GRID_W])
            o_ref[qrows, cols] = (o * _silu(z_ref[qrows, cols].astype(F32))).astype(BF16)

    sc = scores(0)
    for i in range(rows_per_step):
        nxt = scores(i + 1) if i + 1 < rows_per_step else None
        values(i, softmax(i, sc))
        sc = nxt


def _na_bias_table(rpb):
    c = np.arange(GRID_W)
    c0 = np.clip(c - NA_WIN_COLS // 2, 0, GRID_W - NA_WIN_COLS)
    kc = np.arange(GRID_W)
    valid = (kc[None, :] >= c0[:, None]) & (kc[None, :] < c0[:, None] + NA_WIN_COLS)
    dc = kc[None, :] - c[:, None] + NA_WIN_COLS - 1
    n_dc = 2 * NA_WIN_COLS - 1
    n_dr = 2 * NA_WIN_ROWS - 1
    onehot = ((dc[None] == np.arange(n_dc)[:, None, None]) & valid[None]).astype(np.float32)
    toe = jnp.einsum("hrd,dck->hcrk", rpb.astype(F32), onehot, precision=lax.Precision.HIGHEST)
    toe = jnp.where(valid[None, :, None, :], toe, NEG).reshape(NA_HEADS, GRID_W, n_dr * GRID_W)
    n_keys = NA_WIN_ROWS * GRID_W
    tab = jnp.stack([toe[:, :, off * GRID_W:off * GRID_W + n_keys] for off in range(NA_WIN_ROWS)])
    return tab.reshape(NA_WIN_ROWS, NA_WIDTH // NA_PAIR, 2 * GRID_W, n_keys)


def _na(na, meta_na, bias_tab, *, batch, seq):
    n_rows = seq // GRID_W
    rps = NA_ROWS_PER_STEP
    assert n_rows >= NA_WIN_ROWS and seq % GRID_W == 0 and n_rows % rps == 0
    steps = n_rows // rps
    qz = lambda col: pl.BlockSpec((rps * GRID_W, NA_WIDTH), lambda b, r: (b * steps + r, col))
    kv = lambda col: pl.BlockSpec((seq, NA_WIDTH), lambda b, r: (b, col))
    meta = lambda col: pl.BlockSpec((N_META, NA_WIDTH), lambda b, r: (0, col), pipeline_mode=pl.Buffered(1))
    return pl.pallas_call(
        functools.partial(_na_kernel, n_rows=n_rows, rows_per_step=rps),
        grid=(batch, steps),
        in_specs=[qz(0), kv(1), kv(2), qz(3), meta(1), meta(2), _resident(bias_tab)],
        out_specs=qz(0),
        out_shape=jax.ShapeDtypeStruct((batch * seq, NA_WIDTH), BF16),
        compiler_params=_params(2),
        name="na",
    )(na, na, na, na, meta_na, meta_na, bias_tab)


def _conv_silu(x, prev8, next8, w, xs_ref):
    tb = x.shape[0]
    xs_ref[0:8, :] = prev8
    xs_ref[8:tb + 8, :] = x
    xs_ref[tb + 8:tb + 16, :] = next8
    y = (w[0:1] * xs_ref[7:tb + 7, :] + w[1:2] * xs_ref[8:tb + 8, :]
         + w[2:3] * xs_ref[9:tb + 9, :])
    return _silu(y)


def _twice(row):
    return jnp.concatenate([row, row], axis=1)


def _ml_block(q_all, k_all, v_ref, gc_ref, gr_ref, ct_ref, m_ref, emit, *, reverse, fillers=()):
    tb = q_all.shape[0]
    ch = ML_CHUNK
    n_chunks = tb // ch
    order = range(n_chunks - 1, -1, -1) if reverse else range(n_chunks)
    ig_i, b_i = (2 * ML_HEADS, 3 * ML_HEADS) if reverse else (0, ML_HEADS)
    ones = jnp.ones((ch, ML_DH), BF16)
    ri = lax.broadcasted_iota(jnp.int32, (ch, ch), 0)
    ci = lax.broadcasted_iota(jnp.int32, (ch, ch), 1)
    causal = (ci >= ri) if reverse else (ci <= ri)
    eye = ci == ri
    heads = range(ML_HEADS)
    cts = [ct_ref[hd] for hd in heads]
    ms = [m_ref[hd][0:1, :] for hd in heads]
    work = []
    for c in order:
        rows = slice(c * ch, (c + 1) * ch)
        for hd in heads:
            cols = slice(hd * ML_DH, (hd + 1) * ML_DH)
            bc = jnp.broadcast_to(gc_ref[rows, b_i + hd:b_i + hd + 1], (ch, ch))
            r = gr_ref[ig_i + hd:ig_i + hd + 1, rows] - gr_ref[b_i + hd:b_i + hd + 1, rows]
            g = bc[0:1] if reverse else bc[ch - 1:ch]
            logd = jnp.where(causal, bc + jnp.broadcast_to(r, (ch, ch)), NEG)
            rowmax = jnp.max(logd, axis=1, keepdims=True)
            a = g + r
            m = ms[hd]
            m_new = jnp.maximum(g + m, jnp.max(a, axis=1, keepdims=True))
            ms[hd] = m_new
            w_diag = jnp.where(eye, jnp.exp(a - m_new), 0.0).astype(BF16)
            work.append(dict(hd=hd, rows=rows, cols=cols, bc=bc, logd=logd, rowmax=rowmax, m=m,
                             fs=jnp.exp(g + m - m_new), w_diag=w_diag,
                             k=k_all[rows, cols].astype(BF16), q=q_all[rows, cols].astype(BF16),
                             v_ext=jnp.concatenate([v_ref[rows, cols], ones], axis=1)))
    if len(fillers) > 0:
        fillers[0]()
    for w in work:
        w["wv"] = _dot(w["w_diag"], w["v_ext"]).astype(BF16)
    for w in work:
        w["qk"] = _dot_nt(w["q"], w["k"])
    for w in work:
        w["dct"] = _dot_tn(w["k"], w["wv"])
    for w in work:
        w["r2"] = _dot(w["q"], cts[w["hd"]].astype(BF16))
        cts[w["hd"]] = _twice(w["fs"]) * cts[w["hd"]] + w["dct"]
    if len(fillers) > 1:
        fillers[1]()
    for w in work:
        inter = w["bc"] + w["m"]
        m_t = jnp.maximum(w["rowmax"], inter)
        w["e_inter"] = jnp.exp(inter - m_t)
        w["floor"] = jnp.exp(-m_t)
        s = (w["qk"] * jnp.exp(w["logd"] - m_t)).astype(BF16)
        w["r1"] = _dot(s, w["v_ext"])
    for w in work:
        num = w["r1"][:, :ML_DH] + w["e_inter"] * w["r2"][:, :ML_DH]
        den = w["r1"][:, ML_DH:] + w["e_inter"] * w["r2"][:, ML_DH:]
        emit(w["hd"], w["rows"], w["cols"], num / jnp.maximum(jnp.abs(den), w["floor"]))
    for hd in heads:
        ct_ref[hd] = cts[hd]
        m_ref[hd] = jnp.broadcast_to(ms[hd], m_ref.shape[1:])


def _proj_ml_fwd_kernel(x_ref, xnext_ref, g_ref, wna_ref, wqk_ref, wvzo_ref, wmg_ref, wg_ref, wgt_ref,
                        bna_ref, bqk_ref, bvzo_ref, bmg_ref, bg_ref, bgt_ref,
                        mqk_ref, mv_ref, mgc_ref, mgr_ref, cw_ref,
                        na_ref, vzo_ref, mg_ref, gc_ref, gr_ref, hf_ref, qc_ref, kc_ref,
                        ct_ref, m_ref, xs_ref, tail_ref, *, n_blocks):
    i = pl.program_id(1)
    tb = x_ref.shape[0]
    first = i == 0
    last = i == n_blocks - 1

    @pl.when(first)
    def _():
        w = cw_ref[:, ML_WIDTH:2 * ML_WIDTH]
        xs_ref[0:8, :] = jnp.zeros((8, ML_WIDTH), F32)
        xs_ref[8:8 + N_META, :] = mqk_ref[:, ML_WIDTH:2 * ML_WIDTH]
        kcols = slice(ML_WIDTH, 2 * ML_WIDTH)
        head = _dot(_rmsnorm_bf16(x_ref[0:16, :], g_ref[...]), wqk_ref[:, kcols]) + bqk_ref[:, kcols]
        xs_ref[8 + N_META:16 + N_META, :] = head[0:8]
        y = (w[0:1] * xs_ref[7:7 + N_META, :] + w[1:2] * xs_ref[8:8 + N_META, :]
             + w[2:3] * xs_ref[9:9 + N_META, :])
        mk = _silu(y) * (ML_DH ** -0.5)
        ones = jnp.ones((N_META, ML_DH), BF16)
        ri = lax.broadcasted_iota(jnp.int32, (N_META, N_META), 0)
        ci = lax.broadcasted_iota(jnp.int32, (N_META, N_META), 1)
        for hd in range(ML_HEADS):
            cols = slice(hd * ML_DH, (hd + 1) * ML_DH)
            ig = mgc_ref[:, hd:hd + 1]
            lf = mgr_ref[ML_HEADS + hd:ML_HEADS + hd + 1, :]
            a = jnp.sum(jnp.where(ci > ri, lf, 0.0), axis=1, keepdims=True) + ig
            g = jnp.sum(lf, axis=1, keepdims=True)
            m_loc = jnp.max(a, axis=0, keepdims=True)
            m0 = jnp.maximum(g, m_loc)
            v_ext = jnp.concatenate([mv_ref[:, cols], ones], axis=1)
            dct = _dot_tn((jnp.exp(a - m_loc) * mk[:, cols]).astype(BF16), v_ext)
            ct_ref[hd] = jnp.exp(m_loc - m0) * dct
            m_ref[hd] = jnp.broadcast_to(m0, m_ref.shape[1:])
        tail_ref[...] = mqk_ref[8:16, :]

    xb_ext = _rmsnorm_bf16(jnp.concatenate([x_ref[...], xnext_ref[...]], axis=0), g_ref[...])
    xb = xb_ext[0:tb]
    qk_ext = _dot(xb_ext, wqk_ref[...]) + bqk_ref[...]
    vzo_ref[...] = (_dot(xb, wvzo_ref[...]) + bvzo_ref[...]).astype(BF16)
    _store_gate_cumsums(_gate_rows(xb, wgt_ref, bgt_ref), gc_ref, gr_ref, ML_CHUNK)
    qp, kp = qk_ext[:, 0:ML_WIDTH], qk_ext[:, ML_WIDTH:2 * ML_WIDTH]

    prev8 = tail_ref[...]
    zeros8 = jnp.zeros((8, ML_WIDTH), F32)
    qnext = jnp.where(last, zeros8, qp[tb:tb + 8])
    knext = jnp.where(last, zeros8, kp[tb:tb + 8])
    q_all = _conv_silu(qp[0:tb], prev8[:, 0:ML_WIDTH], qnext, cw_ref[:, 0:ML_WIDTH], xs_ref).astype(BF16)
    k_all = _conv_silu(kp[0:tb], prev8[:, ML_WIDTH:2 * ML_WIDTH], knext,
                       cw_ref[:, ML_WIDTH:2 * ML_WIDTH], xs_ref) * (ML_DH ** -0.5)
    tail_ref[...] = qk_ext[tb - 8:tb]
    qc_ref[...] = q_all
    kc_ref[...] = k_all.astype(BF16)

    def emit(hd, rows, cols, h):
        hf_ref[rows, cols] = h

    def project_na():
        na_ref[...] = (_dot(xb, wna_ref[...]) + bna_ref[...]).astype(BF16)

    def project_mg():
        mg_ref[...] = (_dot(xb, wmg_ref[...]) + bmg_ref[...]).astype(BF16)

    _ml_block(q_all, k_all, vzo_ref, gc_ref, gr_ref, ct_ref, m_ref, emit, reverse=False,
              fillers=(project_na, project_mg))


def _merge_project(x, ya_in, yb_in, mg_ref, wa_ref, wb_ref, wo_ref, g_ref):
    ya = _dot(ya_in, wa_ref[...])
    yb = _dot(yb_in, wb_ref[...])
    ga = _sigmoid(mg_ref[:, 0:D_MODEL].astype(F32))
    gb = _sigmoid(mg_ref[:, D_MODEL:2 * D_MODEL].astype(F32))
    y = _dot((ga * ya + gb * yb).astype(BF16), wo_ref[...])
    return x + y * lax.rsqrt(jnp.mean(y * y, axis=-1, keepdims=True) + EPS) * g_ref[...]


def _ml_bwd_out_kernel(qc_ref, kc_ref, v_ref, gc_ref, gr_ref, hf_ref, o_ref, z_ref, hg_ref,
                       x_ref, ya_ref, mg_ref, wa_ref, wb_ref, wo_ref, g_ref, y_ref,
                       ct_ref, m_ref, yb_ref, *, blocks_per_seq):
    s = pl.program_id(0)

    @pl.when(s % blocks_per_seq == 0)
    def _():
        ct_ref[...] = jnp.zeros(ct_ref.shape, F32)
        m_ref[...] = jnp.zeros(m_ref.shape, F32)

    @pl.when(s == 0)
    def _():
        yb_ref[...] = jnp.zeros(yb_ref.shape, BF16)

    y_ref[...] = _merge_project(x_ref[...], ya_ref[...], yb_ref[...], mg_ref, wa_ref, wb_ref, wo_ref, g_ref)

    def emit(hd, rows, cols, h):
        hh = _sigmoid(o_ref[rows, cols].astype(F32)) * (hf_ref[rows, cols] + h)
        mu = jnp.mean(hh, axis=-1, keepdims=True)
        d = hh - mu
        var = jnp.mean(d * d, axis=-1, keepdims=True)
        y = d * lax.rsqrt(var + EPS) * hg_ref[:, cols] * _silu(z_ref[rows, cols].astype(F32))
        yb_ref[rows, cols] = y.astype(BF16)

    _ml_block(qc_ref[...], kc_ref[...], v_ref, gc_ref, gr_ref, ct_ref, m_ref, emit, reverse=True)


def _ml_state_scratch():
    return [pltpu.VMEM((ML_HEADS, ML_DH, 2 * ML_DH), F32), pltpu.VMEM((ML_HEADS, 8, 128), F32)]


def _proj_ml_fwd(x2d, g_pre, wts, meta_qk, meta_vzo, meta_gc, meta_gr, conv_w, *, batch, seq, tb):
    nb = seq // tb
    halo = 16
    rows = batch * seq
    blk = lambda b, i: b * nb + i
    row = lambda width: pl.BlockSpec((tb, width), lambda b, i: (blk(b, i), 0))
    nxt = pl.BlockSpec((halo, D_MODEL),
                       lambda b, i: (jnp.minimum((blk(b, i) + 1) * (tb // halo), rows // halo - 1), 0))
    out_shape = (
        jax.ShapeDtypeStruct((rows, 4 * NA_WIDTH), BF16),
        jax.ShapeDtypeStruct((rows, 3 * ML_WIDTH), BF16),
        jax.ShapeDtypeStruct((rows, 2 * D_MODEL), BF16),
        jax.ShapeDtypeStruct((rows, N_GATES), F32),
        jax.ShapeDtypeStruct((N_GATES, rows), F32),
        jax.ShapeDtypeStruct((rows, ML_WIDTH), F32),
        jax.ShapeDtypeStruct((rows, ML_WIDTH), BF16),
        jax.ShapeDtypeStruct((rows, ML_WIDTH), BF16),
    )
    out_specs = (row(4 * NA_WIDTH), row(3 * ML_WIDTH), row(2 * D_MODEL), row(N_GATES),
                 pl.BlockSpec((N_GATES, tb), lambda b, i: (0, blk(b, i))),
                 row(ML_WIDTH), row(ML_WIDTH), row(ML_WIDTH))
    return pl.pallas_call(
        functools.partial(_proj_ml_fwd_kernel, n_blocks=nb),
        grid=(batch, nb),
        in_specs=[row(D_MODEL), nxt, _resident(g_pre)] + [_resident(a) for a in wts]
                 + [_resident(meta_qk), pl.BlockSpec((N_META, ML_WIDTH), lambda b, i: (0, 0)),
                    _resident(meta_gc), _resident(meta_gr), _resident(conv_w)],
        out_specs=out_specs,
        out_shape=out_shape,
        scratch_shapes=_ml_state_scratch() + [pltpu.VMEM((tb + 16, ML_WIDTH), F32),
                                              pltpu.VMEM((8, 2 * ML_WIDTH), F32)],
        compiler_params=_params(2),
        name="proj_ml_fwd",
    )(x2d, x2d, g_pre, *wts, meta_qk, meta_vzo, meta_gc, meta_gr, conv_w)


def _ml_bwd_out(qc, kc, vzo, gc, gr, hf, head_g, x2d, ya, mg, w_a, w_b, w_out, g_post, *, batch, seq, tb):
    nb = seq // tb
    n = batch * nb

    def row_block(t):
        return (t // nb) * nb + nb - 1 - t % nb

    ml_blk = lambda s: row_block(jnp.minimum(s, n - 1))
    out_blk = lambda s: row_block(jnp.maximum(s, 1) - 1)
    main = lambda col: pl.BlockSpec((tb, ML_WIDTH), lambda s: (ml_blk(s), col))
    wide = lambda width: pl.BlockSpec((tb, width), lambda s: (out_blk(s), 0))
    return pl.pallas_call(
        functools.partial(_ml_bwd_out_kernel, blocks_per_seq=nb),
        grid=(n + 1,),
        in_specs=[main(0), main(0), main(0),
                  pl.BlockSpec((tb, N_GATES), lambda s: (ml_blk(s), 0)),
                  pl.BlockSpec((N_GATES, tb), lambda s: (0, ml_blk(s))),
                  main(0), main(2), main(1), _resident(head_g),
                  wide(D_MODEL), wide(NA_WIDTH), wide(2 * D_MODEL),
                  _resident(w_a), _resident(w_b), _resident(w_out), _resident(g_post)],
        out_specs=wide(D_MODEL),
        out_shape=jax.ShapeDtypeStruct((batch * seq, D_MODEL), F32),
        scratch_shapes=_ml_state_scratch() + [pltpu.VMEM((tb, ML_WIDTH), BF16)],
        compiler_params=_params(1),
        name="ml_bwd_out",
    )(qc, kc, vzo, gc, gr, hf, vzo, vzo, head_g, x2d, ya, mg, w_a, w_b, w_out, g_post)


def _split_in_proj(w_in, b_in):
    o_qk = 4 * NA_WIDTH
    o_vzo = o_qk + 2 * ML_WIDTH
    o_g = o_vzo + 3 * ML_WIDTH
    o_mg = o_g + N_GATES
    b = b_in.astype(F32)[None, :]
    cut = lambda lo, hi: w_in[:, lo:hi].astype(BF16)
    wg = cut(o_g, o_mg)
    bg = b[:, o_g:o_mg]
    return (cut(0, o_qk), cut(o_qk, o_vzo), cut(o_vzo, o_g), cut(o_mg, w_in.shape[1]), wg, wg.T,
            b[:, :o_qk], b[:, o_qk:o_vzo], b[:, o_vzo:o_g], b[:, o_mg:], bg, bg.T)


def _trunk(x, meta_proj, wts, bias_tab, g_pre, conv_w, head_g, w_a, w_b, w_out, g_post):
    batch, seq, _ = x.shape
    assert seq % FWD_BLOCK == 0 and seq % BWD_BLOCK == 0 and FWD_BLOCK % ML_CHUNK == 0 and BWD_BLOCK % ML_CHUNK == 0
    x2d = x.reshape(batch * seq, D_MODEL)
    m_na, m_qk, m_vzo, m_gc, m_gr = meta_proj
    na, vzo, mg, gc, gr, hf, qc, kc = _proj_ml_fwd(x2d, g_pre, wts, m_qk, m_vzo, m_gc, m_gr, conv_w,
                                                   batch=batch, seq=seq, tb=FWD_BLOCK)
    ya = _na(na, m_na, bias_tab, batch=batch, seq=seq)
    y = _ml_bwd_out(qc, kc, vzo, gc, gr, hf, head_g, x2d, ya, mg, w_a, w_b, w_out, g_post,
                    batch=batch, seq=seq, tb=BWD_BLOCK)
    return y.reshape(batch, seq, D_MODEL)


def kernel(x_prompt, x_sample, meta_tokens, g_pre, w_in, b_in, na_rpb, ml_conv_w, ml_head_g, w_a, w_b, w_out, g_post):
    assert w_in.shape[0] == 1, "the meta-token outputs are only droppable for a single layer"
    wts = _split_in_proj(w_in[0], b_in[0])
    gp = g_pre[0].astype(F32)[None, :]
    meta_proj = _meta_proj(meta_tokens.astype(F32), gp, wts)
    bias_tab = _na_bias_table(na_rpb[0])
    args = (meta_proj, wts, bias_tab, gp, ml_conv_w[0].astype(F32), ml_head_g[0].astype(F32)[None, :],
            w_a[0].astype(BF16), w_b[0].astype(BF16), w_out[0].astype(BF16), g_post[0].astype(F32)[None, :])
    return (_trunk(x_prompt, *args), _trunk(x_sample, *args))
```

```python
import functools

import jax
import jax.numpy as jnp
import numpy as np
from jax import lax
from jax.experimental import pallas as pl
from jax.experimental.pallas import tpu as pltpu

D_MODEL = 1024
N_META = 16
GRID_W = 64
EPS = 1e-6
NEG = -1e30
NA_HEADS = 8
NA_DH = 64
NA_WIDTH = NA_HEADS * NA_DH
NA_PAIR = 2 * NA_DH
NA_WIN_ROWS = 8
NA_WIN_COLS = 16
NA_ROWS_PER_STEP = 8
ML_HEADS = 4
ML_DH = 128
ML_WIDTH = ML_HEADS * ML_DH
N_GATES = 4 * ML_HEADS
ML_CHUNK = 128
FWD_BLOCK = 512
BWD_BLOCK = 512
VMEM_LIMIT_BYTES = 56 * 1024 * 1024

F32 = jnp.float32
BF16 = jnp.bfloat16


def _dot(a, b):
    return jnp.dot(a, b, preferred_element_type=F32)


def _dot_nt(a, b):
    return lax.dot_general(a, b, (((1,), (1,)), ((), ())), preferred_element_type=F32)


def _dot_tn(a, b):
    return lax.dot_general(a, b, (((0,), (0,)), ((), ())), preferred_element_type=F32)


def _split3(x):
    hi = x.astype(BF16)
    r1 = x - hi.astype(F32)
    mid = r1.astype(BF16)
    lo = (r1 - mid.astype(F32)).astype(BF16)
    return hi, mid, lo


def _sigmoid(x):
    return 0.5 * jnp.tanh(0.5 * x) + 0.5


def _silu(x):
    return x * _sigmoid(x)


def _log_sigmoid(x):
    return jnp.minimum(x, 0.0) - jnp.log1p(jnp.exp(-jnp.abs(x)))


def _resident(a):
    return pl.BlockSpec(a.shape, lambda *_: (0,) * a.ndim, pipeline_mode=pl.Buffered(1))


def _log2(n):
    assert n & (n - 1) == 0
    return n.bit_length() - 1


def _params(n_grid_axes):
    return pltpu.CompilerParams(dimension_semantics=("arbitrary",) * n_grid_axes,
                                vmem_limit_bytes=VMEM_LIMIT_BYTES)


def _rmsnorm_bf16(x, g):
    return (x * lax.rsqrt(jnp.mean(x * x, axis=-1, keepdims=True) + EPS) * g).astype(BF16)


def _gate_rows(xb, wgt_ref, bgt_ref):
    gr = _dot_nt(wgt_ref[...], xb) + bgt_ref[...]
    kind = lax.broadcasted_iota(jnp.int32, gr.shape, 0) >> _log2(ML_HEADS)
    return jnp.where((kind & 1) == 1, _log_sigmoid(gr), gr)


def _gate_cols(xb, wg_ref, bg_ref):
    gc = _dot(xb, wg_ref[...]) + bg_ref[...]
    kind = lax.broadcasted_iota(jnp.int32, gc.shape, 1) >> _log2(ML_HEADS)
    return jnp.where((kind & 1) == 1, _log_sigmoid(gc), gc)


def _store_gate_cumsums(gr, gc_ref, gr_ref, chunk):
    ri = lax.broadcasted_iota(jnp.int32, (chunk, chunk), 0)
    ci = lax.broadcasted_iota(jnp.int32, (chunk, chunk), 1)
    lower = jnp.where(ci <= ri, 1.0, 0.0).astype(BF16)
    upper = jnp.where(ci >= ri, 1.0, 0.0).astype(BF16)
    kr = lax.broadcasted_iota(jnp.int32, (N_GATES, chunk), 0) >> _log2(ML_HEADS)
    pad = jnp.zeros((chunk - N_GATES, chunk), F32)
    for c in range(gr.shape[1] // chunk):
        rows = slice(c * chunk, (c + 1) * chunk)
        pieces = _split3(gr[:, rows])
        pre_r = sum(_dot(piece, upper) for piece in pieces)
        suf_r = sum(_dot(piece, lower) for piece in pieces)
        out = jnp.where(kr == 1, pre_r, jnp.where(kr == 3, suf_r, gr[:, rows]))
        gr_ref[:, rows] = out
        gc_ref[rows, :] = jnp.concatenate([out, pad], axis=0).T[:, 0:N_GATES]


def _meta_proj_kernel(x_ref, g_ref, wna_ref, wqk_ref, wvzo_ref, wmg_ref, wg_ref, wgt_ref,
                      bna_ref, bqk_ref, bvzo_ref, bmg_ref, bg_ref, bgt_ref,
                      na_ref, qk_ref, vzo_ref, gc_ref, gr_ref):
    xb = _rmsnorm_bf16(x_ref[...], g_ref[...])
    na_ref[...] = (_dot(xb, wna_ref[...]) + bna_ref[...]).astype(BF16)
    qk_ref[...] = _dot(xb, wqk_ref[...]) + bqk_ref[...]
    vzo_ref[...] = (_dot(xb, wvzo_ref[...]) + bvzo_ref[...]).astype(BF16)
    gc_ref[...] = _gate_cols(xb, wg_ref, bg_ref)
    gr_ref[...] = _gate_rows(xb, wgt_ref, bgt_ref)


def _meta_proj(meta, g_pre, wts):
    n = meta.shape[0]
    shapes = ((n, 4 * NA_WIDTH, BF16), (n, 2 * ML_WIDTH, F32), (n, 3 * ML_WIDTH, BF16),
              (n, N_GATES, F32), (N_GATES, n, F32))
    return pl.pallas_call(
        _meta_proj_kernel,
        grid=(1,),
        in_specs=[_resident(meta), _resident(g_pre)] + [_resident(a) for a in wts],
        out_specs=tuple(pl.BlockSpec((r, c), lambda i: (0, 0)) for r, c, _ in shapes),
        out_shape=tuple(jax.ShapeDtypeStruct((r, c), d) for r, c, d in shapes),
        compiler_params=_params(1),
        name="meta_proj",
    )(meta, g_pre, *wts)


def _na_row_start(r, n_rows):
    return jnp.clip(r - NA_WIN_ROWS // 2, 0, n_rows - NA_WIN_ROWS)


def _na_kernel(q_ref, k_ref, v_ref, z_ref, mk_ref, mv_ref, bias_ref, o_ref, *, n_rows, rows_per_step):
    n_keys = NA_WIN_ROWS * GRID_W
    scale = NA_DH ** -0.5
    lane = lax.broadcasted_iota(jnp.int32, (GRID_W, NA_PAIR), 1)
    lo = lane < NA_DH
    n_pairs = NA_WIDTH // NA_PAIR
    groups = [slice(g * NA_PAIR, (g + 1) * NA_PAIR) for g in range(n_pairs)]

    def window(i):
        r = pl.program_id(1) * rows_per_step + i
        r0 = _na_row_start(r, n_rows)
        return pl.multiple_of(r0 * GRID_W, GRID_W), r0 - r + NA_WIN_ROWS - 1

    def scores(i):
        start, _ = window(i)
        qrows = slice(i * GRID_W, (i + 1) * GRID_W)
        out = []
        for cols in groups:
            qf = q_ref[qrows, cols].astype(F32) * scale
            q2 = jnp.concatenate([jnp.where(lo, qf, 0.0), jnp.where(lo, 0.0, qf)], axis=0).astype(BF16)
            s = _dot_nt(q2, k_ref[pl.ds(start, n_keys), cols])
            sm = _dot_nt(q2, mk_ref[:, cols])
            out.append((s, sm))
        return out

    def softmax(i, sc):
        _, off = window(i)
        out = []
        for g, (s, sm) in enumerate(sc):
            bias = bias_ref[off, g]
            s = jnp.where(bias > 0.5 * NEG, s + bias, NEG)
            m = jnp.maximum(jnp.max(s, axis=-1, keepdims=True), jnp.max(sm, axis=-1, keepdims=True))
            p = jnp.exp(s - m)
            pm = jnp.exp(sm - m)
            denom = jnp.sum(p, axis=-1, keepdims=True) + jnp.sum(pm, axis=-1, keepdims=True)
            out.append((p.astype(BF16), pm.astype(BF16), denom))
        return out

    def values(i, probs):
        start, _ = window(i)
        qrows = slice(i * GRID_W, (i + 1) * GRID_W)
        for cols, (p, pm, denom) in zip(groups, probs):
            o2 = _dot(p, v_ref[pl.ds(start, n_keys), cols]) + _dot(pm, mv_ref[:, cols])
            o2 = o2 / denom
            o = jnp.where(lo, o2[0:GRID_W], o2[GRID_W:2 * GRID_W])
            o_ref[qrows, cols] = (o * _silu(z_ref[qrows, cols].astype(F32))).astype(BF16)

    sc = scores(0)
    for i in range(rows_per_step):
        nxt = scores(i + 1) if i + 1 < rows_per_step else None
        values(i, softmax(i, sc))
        sc = nxt


def _na_bias_table(rpb):
    c = np.arange(GRID_W)
    c0 = np.clip(c - NA_WIN_COLS // 2, 0, GRID_W - NA_WIN_COLS)
    kc = np.arange(GRID_W)
    valid = (kc[None, :] >= c0[:, None]) & (kc[None, :] < c0[:, None] + NA_WIN_COLS)
    dc = kc[None, :] - c[:, None] + NA_WIN_COLS - 1
    n_dc = 2 * NA_WIN_COLS - 1
    n_dr = 2 * NA_WIN_ROWS - 1
    onehot = ((dc[None] == np.arange(n_dc)[:, None, None]) & valid[None]).astype(np.float32)
    toe = jnp.einsum("hrd,dck->hcrk", rpb.astype(F32), onehot, precision=lax.Precision.HIGHEST)
    toe = jnp.where(valid[None, :, None, :], toe, NEG).reshape(NA_HEADS, GRID_W, n_dr * GRID_W)
    n_keys = NA_WIN_ROWS * GRID_W
    tab = jnp.stack([toe[:, :, off * GRID_W:off * GRID_W + n_keys] for off in range(NA_WIN_ROWS)])
    return tab.reshape(NA_WIN_ROWS, NA_WIDTH // NA_PAIR, 2 * GRID_W, n_keys)


def _na(na, meta_na, bias_tab, *, batch, seq):
    n_rows = seq // GRID_W
    rps = NA_ROWS_PER_STEP
    assert n_rows >= NA_WIN_ROWS and seq % GRID_W == 0 and n_rows % rps == 0
    steps = n_rows // rps
    qz = lambda col: pl.BlockSpec((rps * GRID_W, NA_WIDTH), lambda b, r: (b * steps + r, col))
    kv = lambda col: pl.BlockSpec((seq, NA_WIDTH), lambda b, r: (b, col))
    meta = lambda col: pl.BlockSpec((N_META, NA_WIDTH), lambda b, r: (0, col), pipeline_mode=pl.Buffered(1))
    return pl.pallas_call(
        functools.partial(_na_kernel, n_rows=n_rows, rows_per_step=rps),
        grid=(batch, steps),
        in_specs=[qz(0), kv(1), kv(2), qz(3), meta(1), meta(2), _resident(bias_tab)],
        out_specs=qz(0),
        out_shape=jax.ShapeDtypeStruct((batch * seq, NA_WIDTH), BF16),
        compiler_params=_params(2),
        name="na",
    )(na, na, na, na, meta_na, meta_na, bias_tab)


def _conv_silu(x, prev8, next8, w, xs_ref):
    tb = x.shape[0]
    xs_ref[0:8, :] = prev8
    xs_ref[8:tb + 8, :] = x
    xs_ref[tb + 8:tb + 16, :] = next8
    y = (w[0:1] * xs_ref[7:tb + 7, :] + w[1:2] * xs_ref[8:tb + 8, :]
         + w[2:3] * xs_ref[9:tb + 9, :])
    return _silu(y)


def _twice(row):
    return jnp.concatenate([row, row], axis=1)


def _ml_block(q_all, k_all, v_ref, gc_ref, gr_ref, ct_ref, m_ref, emit, *, reverse, fillers=(),
              weights_on_mxu=True):
    tb = q_all.shape[0]
    ch = ML_CHUNK
    n_chunks = tb // ch
    order = range(n_chunks - 1, -1, -1) if reverse else range(n_chunks)
    ig_i, b_i = (2 * ML_HEADS, 3 * ML_HEADS) if reverse else (0, ML_HEADS)
    ones = jnp.ones((ch, ML_DH), BF16)
    ri = lax.broadcasted_iota(jnp.int32, (ch, ch), 0)
    ci = lax.broadcasted_iota(jnp.int32, (ch, ch), 1)
    causal = (ci >= ri) if reverse else (ci <= ri)
    eye = ci == ri
    heads = range(ML_HEADS)
    cts = [ct_ref[hd] for hd in heads]
    ms = [m_ref[hd][0:1, :] for hd in heads]
    work = []
    for c in order:
        rows = slice(c * ch, (c + 1) * ch)
        for hd in heads:
            cols = slice(hd * ML_DH, (hd + 1) * ML_DH)
            bc = jnp.broadcast_to(gc_ref[rows, b_i + hd:b_i + hd + 1], (ch, ch))
            r = gr_ref[ig_i + hd:ig_i + hd + 1, rows] - gr_ref[b_i + hd:b_i + hd + 1, rows]
            g = bc[0:1] if reverse else bc[ch - 1:ch]
            logd = jnp.where(causal, bc + jnp.broadcast_to(r, (ch, ch)), NEG)
            rowmax = jnp.max(logd, axis=1, keepdims=True)
            a = g + r
            m = ms[hd]
            m_new = jnp.maximum(g + m, jnp.max(a, axis=1, keepdims=True))
            ms[hd] = m_new
            k = k_all[rows, cols]
            if weights_on_mxu:
                w_src = jnp.where(eye, jnp.exp(a - m_new), 0.0).astype(BF16)
            else:
                ic = jnp.broadcast_to(gc_ref[rows, ig_i + hd:ig_i + hd + 1], (ch, ch))
                w_src = (jnp.exp(g - bc + ic - m_new) * k.astype(F32)).astype(BF16)
            work.append(dict(hd=hd, rows=rows, cols=cols, bc=bc, logd=logd, rowmax=rowmax, m=m,
                             fs=jnp.exp(g + m - m_new), w_src=w_src,
                             k=k.astype(BF16), q=q_all[rows, cols].astype(BF16),
                             v_ext=jnp.concatenate([v_ref[rows, cols], ones], axis=1)))
    if len(fillers) > 0:
        fillers[0]()
    if weights_on_mxu:
        for w in work:
            w["wv"] = _dot(w["w_src"], w["v_ext"]).astype(BF16)
    for w in work:
        w["qk"] = _dot_nt(w["q"], w["k"])
    for w in work:
        w["dct"] = _dot_tn(w["k"], w["wv"]) if weights_on_mxu else _dot_tn(w["w_src"], w["v_ext"])
    for w in work:
        w["r2"] = _dot(w["q"], cts[w["hd"]].astype(BF16))
        cts[w["hd"]] = _twice(w["fs"]) * cts[w["hd"]] + w["dct"]
    if len(fillers) > 1:
        fillers[1]()
    for w in work:
        inter = w["bc"] + w["m"]
        m_t = jnp.maximum(w["rowmax"], inter)
        w["e_inter"] = jnp.exp(inter - m_t)
        w["floor"] = jnp.exp(-m_t)
        s = (w["qk"] * jnp.exp(w["logd"] - m_t)).astype(BF16)
        w["r1"] = _dot(s, w["v_ext"])
    for w in work:
        num = w["r1"][:, :ML_DH] + w["e_inter"] * w["r2"][:, :ML_DH]
        den = w["r1"][:, ML_DH:] + w["e_inter"] * w["r2"][:, ML_DH:]
        emit(w["hd"], w["rows"], w["cols"], num / jnp.maximum(jnp.abs(den), w["floor"]))
    for hd in heads:
        ct_ref[hd] = cts[hd]
        m_ref[hd] = jnp.broadcast_to(ms[hd], m_ref.shape[1:])


def _proj_ml_fwd_kernel(x_ref, xnext_ref, g_ref, wna_ref, wqk_ref, wvzo_ref, wmg_ref, wg_ref, wgt_ref,
                        bna_ref, bqk_ref, bvzo_ref, bmg_ref, bg_ref, bgt_ref,
                        mqk_ref, mv_ref, mgc_ref, mgr_ref, cw_ref,
                        na_ref, vzo_ref, mg_ref, gc_ref, gr_ref, hf_ref, qc_ref, kc_ref,
                        ct_ref, m_ref, xs_ref, tail_ref, *, n_blocks):
    i = pl.program_id(1)
    tb = x_ref.shape[0]
    first = i == 0
    last = i == n_blocks - 1

    @pl.when(first)
    def _():
        w = cw_ref[:, ML_WIDTH:2 * ML_WIDTH]
        xs_ref[0:8, :] = jnp.zeros((8, ML_WIDTH), F32)
        xs_ref[8:8 + N_META, :] = mqk_ref[:, ML_WIDTH:2 * ML_WIDTH]
        kcols = slice(ML_WIDTH, 2 * ML_WIDTH)
        head = _dot(_rmsnorm_bf16(x_ref[0:16, :], g_ref[...]), wqk_ref[:, kcols]) + bqk_ref[:, kcols]
        xs_ref[8 + N_META:16 + N_META, :] = head[0:8]
        y = (w[0:1] * xs_ref[7:7 + N_META, :] + w[1:2] * xs_ref[8:8 + N_META, :]
             + w[2:3] * xs_ref[9:9 + N_META, :])
        mk = _silu(y) * (ML_DH ** -0.5)
        ones = jnp.ones((N_META, ML_DH), BF16)
        ri = lax.broadcasted_iota(jnp.int32, (N_META, N_META), 0)
        ci = lax.broadcasted_iota(jnp.int32, (N_META, N_META), 1)
        for hd in range(ML_HEADS):
            cols = slice(hd * ML_DH, (hd + 1) * ML_DH)
            ig = mgc_ref[:, hd:hd + 1]
            lf = mgr_ref[ML_HEADS + hd:ML_HEADS + hd + 1, :]
            a = jnp.sum(jnp.where(ci > ri, lf, 0.0), axis=1, keepdims=True) + ig
            g = jnp.sum(lf, axis=1, keepdims=True)
            m_loc = jnp.max(a, axis=0, keepdims=True)
            m0 = jnp.maximum(g, m_loc)
            v_ext = jnp.concatenate([mv_ref[:, cols], ones], axis=1)
            dct = _dot_tn((jnp.exp(a - m_loc) * mk[:, cols]).astype(BF16), v_ext)
            ct_ref[hd] = jnp.exp(m_loc - m0) * dct
            m_ref[hd] = jnp.broadcast_to(m0, m_ref.shape[1:])
        tail_ref[...] = mqk_ref[8:16, :]

    xb_ext = _rmsnorm_bf16(jnp.concatenate([x_ref[...], xnext_ref[...]], axis=0), g_ref[...])
    xb = xb_ext[0:tb]
    qk_ext = _dot(xb_ext, wqk_ref[...]) + bqk_ref[...]
    vzo_ref[...] = (_dot(xb, wvzo_ref[...]) + bvzo_ref[...]).astype(BF16)
    _store_gate_cumsums(_gate_rows(xb, wgt_ref, bgt_ref), gc_ref, gr_ref, ML_CHUNK)
    qp, kp = qk_ext[:, 0:ML_WIDTH], qk_ext[:, ML_WIDTH:2 * ML_WIDTH]

    prev8 = tail_ref[...]
    zeros8 = jnp.zeros((8, ML_WIDTH), F32)
    qnext = jnp.where(last, zeros8, qp[tb:tb + 8])
    knext = jnp.where(last, zeros8, kp[tb:tb + 8])
    q_all = _conv_silu(qp[0:tb], prev8[:, 0:ML_WIDTH], qnext, cw_ref[:, 0:ML_WIDTH], xs_ref).astype(BF16)
    k_all = _conv_silu(kp[0:tb], prev8[:, ML_WIDTH:2 * ML_WIDTH], knext,
                       cw_ref[:, ML_WIDTH:2 * ML_WIDTH], xs_ref) * (ML_DH ** -0.5)
    tail_ref[...] = qk_ext[tb - 8:tb]
    qc_ref[...] = q_all
    kc_ref[...] = k_all.astype(BF16)

    def emit(hd, rows, cols, h):
        hf_ref[rows, cols] = h

    def project_na():
        na_ref[...] = (_dot(xb, wna_ref[...]) + bna_ref[...]).astype(BF16)

    def project_mg():
        mg_ref[...] = (_dot(xb, wmg_ref[...]) + bmg_ref[...]).astype(BF16)

    _ml_block(q_all, k_all, vzo_ref, gc_ref, gr_ref, ct_ref, m_ref, emit, reverse=False,
              fillers=(project_na, project_mg), weights_on_mxu=False)


def _merge_project(x, ya_in, yb_in, mg_ref, wa_ref, wb_ref, wo_ref, g_ref):
    ya = _dot(ya_in, wa_ref[...])
    yb = _dot(yb_in, wb_ref[...])
    ga = _sigmoid(mg_ref[:, 0:D_MODEL].astype(F32))
    gb = _sigmoid(mg_ref[:, D_MODEL:2 * D_MODEL].astype(F32))
    y = _dot((ga * ya + gb * yb).astype(BF16), wo_ref[...])
    return x + y * lax.rsqrt(jnp.mean(y * y, axis=-1, keepdims=True) + EPS) * g_ref[...]


def _ml_bwd_out_kernel(qc_ref, kc_ref, v_ref, gc_ref, gr_ref, hf_ref, o_ref, z_ref, hg_ref,
                       x_ref, ya_ref, mg_ref, wa_ref, wb_ref, wo_ref, g_ref, y_ref,
                       ct_ref, m_ref, yb_ref, *, blocks_per_seq):
    s = pl.program_id(0)

    @pl.when(s % blocks_per_seq == 0)
    def _():
        ct_ref[...] = jnp.zeros(ct_ref.shape, F32)
        m_ref[...] = jnp.zeros(m_ref.shape, F32)

    @pl.when(s == 0)
    def _():
        yb_ref[...] = jnp.zeros(yb_ref.shape, BF16)

    y_ref[...] = _merge_project(x_ref[...], ya_ref[...], yb_ref[...], mg_ref, wa_ref, wb_ref, wo_ref, g_ref)

    def emit(hd, rows, cols, h):
        hh = _sigmoid(o_ref[rows, cols].astype(F32)) * (hf_ref[rows, cols] + h)
        mu = jnp.mean(hh, axis=-1, keepdims=True)
        d = hh - mu
        var = jnp.mean(d * d, axis=-1, keepdims=True)
        y = d * lax.rsqrt(var + EPS) * hg_ref[:, cols] * _silu(z_ref[rows, cols].astype(F32))
        yb_ref[rows, cols] = y.astype(BF16)

    _ml_block(qc_ref[...], kc_ref[...], v_ref, gc_ref, gr_ref, ct_ref, m_ref, emit, reverse=True)


def _ml_state_scratch():
    return [pltpu.VMEM((ML_HEADS, ML_DH, 2 * ML_DH), F32), pltpu.VMEM((ML_HEADS, 8, 128), F32)]


def _proj_ml_fwd(x2d, g_pre, wts, meta_qk, meta_vzo, meta_gc, meta_gr, conv_w, *, batch, seq, tb):
    nb = seq // tb
    halo = 16
    rows = batch * seq
    blk = lambda b, i: b * nb + i
    row = lambda width: pl.BlockSpec((tb, width), lambda b, i: (blk(b, i), 0))
    nxt = pl.BlockSpec((halo, D_MODEL),
                       lambda b, i: (jnp.minimum((blk(b, i) + 1) * (tb // halo), rows // halo - 1), 0))
    out_shape = (
        jax.ShapeDtypeStruct((rows, 4 * NA_WIDTH), BF16),
        jax.ShapeDtypeStruct((rows, 3 * ML_WIDTH), BF16),
        jax.ShapeDtypeStruct((rows, 2 * D_MODEL), BF16),
        jax.ShapeDtypeStruct((rows, N_GATES), F32),
        jax.ShapeDtypeStruct((N_GATES, rows), F32),
        jax.ShapeDtypeStruct((rows, ML_WIDTH), F32),
        jax.ShapeDtypeStruct((rows, ML_WIDTH), BF16),
        jax.ShapeDtypeStruct((rows, ML_WIDTH), BF16),
    )
    out_specs = (row(4 * NA_WIDTH), row(3 * ML_WIDTH), row(2 * D_MODEL), row(N_GATES),
                 pl.BlockSpec((N_GATES, tb), lambda b, i: (0, blk(b, i))),
                 row(ML_WIDTH), row(ML_WIDTH), row(ML_WIDTH))
    return pl.pallas_call(
        functools.partial(_proj_ml_fwd_kernel, n_blocks=nb),
        grid=(batch, nb),
        in_specs=[row(D_MODEL), nxt, _resident(g_pre)] + [_resident(a) for a in wts]
                 + [_resident(meta_qk), pl.BlockSpec((N_META, ML_WIDTH), lambda b, i: (0, 0)),
                    _resident(meta_gc), _resident(meta_gr), _resident(conv_w)],
        out_specs=out_specs,
        out_shape=out_shape,
        scratch_shapes=_ml_state_scratch() + [pltpu.VMEM((tb + 16, ML_WIDTH), F32),
                                              pltpu.VMEM((8, 2 * ML_WIDTH), F32)],
        compiler_params=_params(2),
        name="proj_ml_fwd",
    )(x2d, x2d, g_pre, *wts, meta_qk, meta_vzo, meta_gc, meta_gr, conv_w)


def _ml_bwd_out(qc, kc, vzo, gc, gr, hf, head_g, x2d, ya, mg, w_a, w_b, w_out, g_post, *, batch, seq, tb):
    nb = seq // tb
    n = batch * nb

    def row_block(t):
        return (t // nb) * nb + nb - 1 - t % nb

    ml_blk = lambda s: row_block(jnp.minimum(s, n - 1))
    out_blk = lambda s: row_block(jnp.maximum(s, 1) - 1)
    main = lambda col: pl.BlockSpec((tb, ML_WIDTH), lambda s: (ml_blk(s), col))
    wide = lambda width: pl.BlockSpec((tb, width), lambda s: (out_blk(s), 0))
    return pl.pallas_call(
        functools.partial(_ml_bwd_out_kernel, blocks_per_seq=nb),
        grid=(n + 1,),
        in_specs=[main(0), main(0), main(0),
                  pl.BlockSpec((tb, N_GATES), lambda s: (ml_blk(s), 0)),
                  pl.BlockSpec((N_GATES, tb), lambda s: (0, ml_blk(s))),
                  main(0), main(2), main(1), _resident(head_g),
                  wide(D_MODEL), wide(NA_WIDTH), wide(2 * D_MODEL),
                  _resident(w_a), _resident(w_b), _resident(w_out), _resident(g_post)],
        out_specs=wide(D_MODEL),
        out_shape=jax.ShapeDtypeStruct((batch * seq, D_MODEL), F32),
        scratch_shapes=_ml_state_scratch() + [pltpu.VMEM((tb, ML_WIDTH), BF16)],
        compiler_params=_params(1),
        name="ml_bwd_out",
    )(qc, kc, vzo, gc, gr, hf, vzo, vzo, head_g, x2d, ya, mg, w_a, w_b, w_out, g_post)


def _split_in_proj(w_in, b_in):
    o_qk = 4 * NA_WIDTH
    o_vzo = o_qk + 2 * ML_WIDTH
    o_g = o_vzo + 3 * ML_WIDTH
    o_mg = o_g + N_GATES
    b = b_in.astype(F32)[None, :]
    cut = lambda lo, hi: w_in[:, lo:hi].astype(BF16)
    wg = cut(o_g, o_mg)
    bg = b[:, o_g:o_mg]
    return (cut(0, o_qk), cut(o_qk, o_vzo), cut(o_vzo, o_g), cut(o_mg, w_in.shape[1]), wg, wg.T,
            b[:, :o_qk], b[:, o_qk:o_vzo], b[:, o_vzo:o_g], b[:, o_mg:], bg, bg.T)


def _trunk(x, meta_proj, wts, bias_tab, g_pre, conv_w, head_g, w_a, w_b, w_out, g_post):
    batch, seq, _ = x.shape
    assert seq % FWD_BLOCK == 0 and seq % BWD_BLOCK == 0 and FWD_BLOCK % ML_CHUNK == 0 and BWD_BLOCK % ML_CHUNK == 0
    x2d = x.reshape(batch * seq, D_MODEL)
    m_na, m_qk, m_vzo, m_gc, m_gr = meta_proj
    na, vzo, mg, gc, gr, hf, qc, kc = _proj_ml_fwd(x2d, g_pre, wts, m_qk, m_vzo, m_gc, m_gr, conv_w,
                                                   batch=batch, seq=seq, tb=FWD_BLOCK)
    ya = _na(na, m_na, bias_tab, batch=batch, seq=seq)
    y = _ml_bwd_out(qc, kc, vzo, gc, gr, hf, head_g, x2d, ya, mg, w_a, w_b, w_out, g_post,
                    batch=batch, seq=seq, tb=BWD_BLOCK)
    return y.reshape(batch, seq, D_MODEL)


def kernel(x_prompt, x_sample, meta_tokens, g_pre, w_in, b_in, na_rpb, ml_conv_w, ml_head_g, w_a, w_b, w_out, g_post):
    assert w_in.shape[0] == 1, "the meta-token outputs are only droppable for a single layer"
    wts = _split_in_proj(w_in[0], b_in[0])
    gp = g_pre[0].astype(F32)[None, :]
    meta_proj = _meta_proj(meta_tokens.astype(F32), gp, wts)
    bias_tab = _na_bias_table(na_rpb[0])
    args = (meta_proj, wts, bias_tab, gp, ml_conv_w[0].astype(F32), ml_head_g[0].astype(F32)[None, :],
            w_a[0].astype(BF16), w_b[0].astype(BF16), w_out[0].astype(BF16), g_post[0].astype(F32)[None, :])
    return (_trunk(x_prompt, *args), _trunk(x_sample, *args))
```
